```python
import jax
import jax.numpy as jnp
from jax import lax
import numpy as np

D_MODEL = 1024
BATCH = 16
SEQ = 2048
DEPTH = 4

GRID_W = 64
CTX_LEN = 256
D_A = D_MODEL // 2
D_B = D_MODEL // 2
HEAD_B = 64
H_B = D_B // HEAD_B
R_W = 64
R_A = 64
R_G = 128
N_GROUPS = 4
EXP_PER_GROUP = 8
N_EXPERTS = N_GROUPS * EXP_PER_GROUP
TOP_K = 2
D_FF_EXPERT = D_MODEL // 2
MOE_BLOCK = 128
NORM_EPS = 1e-6
GN_EPS = 64e-5
RW_COLS = 3 * D_B + 2 * R_W + 2 * R_A + R_G
RW_START = 3 * D_A
N_IN = RW_START + RW_COLS + 2 * D_MODEL
IN_SPLITS = (D_A, 2 * D_A, RW_START, RW_START + RW_COLS, RW_START + RW_COLS + D_MODEL)
RW_SPLITS = (D_B, 2 * D_B, 3 * D_B, 3 * D_B + R_W, 3 * D_B + 2 * R_W,
             3 * D_B + 2 * R_W + R_A, 3 * D_B + 2 * R_W + 2 * R_A)

kernel_name = "hybrid_conv_rwkv7_hmoe_diffusion_trunk"


def _rmsnorm(x, g):
    xf = x.astype(jnp.float32)
    y = xf * lax.rsqrt(jnp.mean(xf * xf, axis=-1, keepdims=True) + NORM_EPS)
    return y.astype(x.dtype) * g


def _dwconv3(u, w, axis):
    pad = [(0, 0)] * u.ndim
    pad[axis] = (1, 1)
    up = jnp.pad(u, pad)
    n = u.shape[axis]
    sl = lambda i: lax.slice_in_dim(up, i, i + n, axis=axis)
    return sl(0) * w[0] + sl(1) * w[1] + sl(2) * w[2]


def _conv_grid(u, w):
    bsz, s, ch = u.shape
    rows = s // GRID_W
    g = u.reshape(bsz, rows, GRID_W, ch)
    half = ch // 2
    horiz = _dwconv3(g[..., :half], w[:, :half], axis=2)
    vert = _dwconv3(g[..., half:], w[:, half:], axis=1)
    return jnp.concatenate([horiz, vert], axis=-1).reshape(bsz, s, ch)


def _token_shift(z, mu):
    zp = jnp.pad(z, ((0, 0), (1, 1), (0, 0)))
    return z + mu * (0.5 * (zp[:, :-2] + zp[:, 2:]) - z)


def _rwkv_prep(zrw, lp):
    bsz, t, _ = zrw.shape
    hd = lambda u: u.astype(jnp.float32).reshape(bsz, t, H_B, HEAD_B)
    r, k, v, zwf, zwb, zaf, zab, zg = jnp.split(zrw, RW_SPLITS, axis=-1)
    kk = hd(k * lp["k_k"])
    kk = kk * lax.rsqrt(jnp.sum(kk * kk, axis=-1, keepdims=True) + 1e-12)
    k_a = lp["k_a"].astype(jnp.float32).reshape(H_B, HEAD_B)
    kh = hd(k)
    dirs = []
    for d, (zw, za) in enumerate(((zwf, zaf), (zwb, zab))):
        wl = -jax.nn.softplus(-hd(lp["w0"][d] + jnp.tanh(zw) @ lp["w_up"][d])) - 0.5
        a = jax.nn.sigmoid(hd(lp["a0"][d] + za @ lp["a_up"][d]))
        kd = kh * (1.0 + (a - 1.0) * k_a)
        dirs.append((jnp.exp(-jnp.exp(wl)), kd, kk * a))
    return hd(r), hd(v), kk, dirs, zg


def _wkv_scan(s0, r, decay, k, v, kk, b, reverse):
    def step(s, inp):
        w_t, k_t, v_t, kk_t, b_t = inp[:5]
        sa = jnp.einsum("bhvk,bhk->bhv", s, kk_t)
        s = (s * w_t[:, :, None, :] - sa[..., None] * b_t[:, :, None, :]
             + v_t[..., None] * k_t[:, :, None, :])
        if r is None:
            return s, None
        return s, jnp.einsum("bhvk,bhk->bhv", s, inp[5])
    seqs = (decay, k, v, kk, b) + (() if r is None else (r,))
    xs = tuple(jnp.swapaxes(u, 0, 1) for u in seqs)
    s_fin, ys = lax.scan(step, s0, xs, reverse=reverse)
    return s_fin, (None if r is None else jnp.swapaxes(ys, 0, 1))


def _rwkv_readout(y, r, v, kds, zg, lp):
    bsz, t = y.shape[:2]
    mu = jnp.mean(y, axis=-1, keepdims=True)
    var = jnp.mean(jnp.square(y - mu), axis=-1, keepdims=True)
    yn = ((y - mu) * lax.rsqrt(var + GN_EPS)).reshape(bsz, t, D_B) * lp["lnx_g"] + lp["lnx_b"]
    rk = lp["r_k"].astype(jnp.float32)
    bonus = sum(jnp.sum(r * kd * rk, axis=-1, keepdims=True) for kd in kds) * v
    g = jax.nn.sigmoid(zg) @ lp["g_up"]
    out = ((yn + bonus.reshape(bsz, t, D_B)) * g).astype(zg.dtype)
    return out @ lp["w_b_out"]


def _token_mixer(h_lat, h_ctx, lp, with_ctx_out):
    bsz = h_lat.shape[0]
    bg, cg, ha, rw, ga, gb = jnp.split(h_lat @ lp["w_in"], IN_SPLITS, axis=-1)
    if with_ctx_out:
        bgc, cgc, hac, rwc, gac, gbc = jnp.split(h_ctx @ lp["w_in"], IN_SPLITS, axis=-1)
    else:
        rwc = h_ctx @ lp["w_in"][:, RW_START:RW_START + RW_COLS]
    r_l, v_l, kk_l, dirs_l, zg_l = _rwkv_prep(_token_shift(rw, lp["shift_mu"]), lp)
    r_c, v_c, kk_c, dirs_c, zg_c = _rwkv_prep(_token_shift(rwc, lp["shift_mu"]), lp)
    s0 = jnp.zeros((bsz, H_B, HEAD_B, HEAD_B), jnp.float32)
    y_l = 0.0
    y_c = 0.0
    for d in range(2):
        dec_c, k_c, b_c = dirs_c[d]
        s_c, yc_d = _wkv_scan(s0, r_c if with_ctx_out else None, dec_c, k_c, v_c, kk_c, b_c, d == 1)
        dec_l, k_l, b_l = dirs_l[d]
        _, yl_d = _wkv_scan(s_c, r_l, dec_l, k_l, v_l, kk_l, b_l, d == 1)
        y_l = y_l + yl_d
        if with_ctx_out:
            y_c = y_c + yc_d
    ya = (bg * _conv_grid(cg * ha, lp["conv_w"])) @ lp["w_a_out"]
    yb = _rwkv_readout(y_l, r_l, v_l, [dl[1] for dl in dirs_l], zg_l, lp)
    y_lat = (jax.nn.sigmoid(ga) * ya + jax.nn.sigmoid(gb) * yb) @ lp["w_o"]
    if not with_ctx_out:
        return y_lat, None
    yac = (bgc * _dwconv3(cgc * hac, lp["conv_w"], axis=1)) @ lp["w_a_out"]
    ybc = _rwkv_readout(y_c, r_c, v_c, [dc[1] for dc in dirs_c], zg_c, lp)
    y_ctx = (jax.nn.sigmoid(gac) * yac + jax.nn.sigmoid(gbc) * ybc) @ lp["w_o"]
    return y_lat, y_ctx


def _hier_moe(h, lp):
    n, dm = h.shape
    lg = (h @ lp["router_g"] + lp["router_g_b"]).astype(jnp.float32)
    p_g = jax.nn.softmax(lg, axis=-1)
    g_sel = jnp.argmax(lg, axis=-1).astype(jnp.int32)
    le = (h @ lp["router_e"] + lp["router_e_b"]).astype(jnp.float32).reshape(n, N_GROUPS, EXP_PER_GROUP)
    le = jnp.take_along_axis(le, jnp.broadcast_to(g_sel[:, None, None], (n, 1, EXP_PER_GROUP)), axis=1)[:, 0]
    top_p, top_i = lax.top_k(jax.nn.softmax(le, axis=-1), TOP_K)
    gate = jnp.max(p_g, axis=-1, keepdims=True) * top_p / jnp.sum(top_p, axis=-1, keepdims=True)
    expert = (g_sel[:, None] * EXP_PER_GROUP + top_i).reshape(-1).astype(jnp.int32)
    m = n * TOP_K
    token = jnp.repeat(jnp.arange(n, dtype=jnp.int32), TOP_K)
    order = jnp.argsort(expert)
    e_sorted = expert[order]
    counts = jnp.bincount(expert, length=N_EXPERTS)
    padded = (counts + MOE_BLOCK - 1) // MOE_BLOCK * MOE_BLOCK
    pad_end = jnp.cumsum(padded)
    dest = (pad_end - padded)[e_sorted] + jnp.arange(m) - (jnp.cumsum(counts) - counts)[e_sorted]
    n_blocks = -(-m // MOE_BLOCK) + N_EXPERTS
    slot_tok = jnp.full((n_blocks * MOE_BLOCK,), n, jnp.int32).at[dest].set(token[order])
    slot_gate = jnp.zeros((n_blocks * MOE_BLOCK,), jnp.float32).at[dest].set(gate.reshape(-1)[order])
    block_exp = jnp.minimum(
        jnp.searchsorted(pad_end, jnp.arange(n_blocks) * MOE_BLOCK, side="right"), N_EXPERTS - 1)
    h_pad = jnp.concatenate([h, jnp.zeros((1, dm), h.dtype)], axis=0)

    def expert_block(args):
        tok, e = args
        xb = h_pad[tok]
        hid = jax.nn.silu(xb @ lp["exp_w1"][e]) * (xb @ lp["exp_w3"][e])
        return hid @ lp["exp_w2"][e]

    yb = lax.map(expert_block, (slot_tok.reshape(n_blocks, MOE_BLOCK), block_exp))
    yb = yb.reshape(-1, dm) * slot_gate[:, None].astype(h.dtype)
    return jnp.zeros((n + 1, dm), h.dtype).at[slot_tok].add(yb)[:n]


def setup_inputs(seed: int = 0) -> dict:
    key = jax.random.key(seed)
    ks = iter(jax.random.split(key, 40))
    nrm = lambda shape, s: jax.random.normal(next(ks), shape, jnp.float32) * s
    L, D = DEPTH, D_MODEL
    return {
        "x": nrm((BATCH, SEQ, D), 1.0),
        "c": nrm((BATCH, D), 1.0),
        "ctx": nrm((BATCH, CTX_LEN, D), 1.0),
        "c_ctx": nrm((D,), 1.0),
        "w_mod": nrm((L, D, 6 * D), 0.5 * D ** -0.5),
        "b_mod": nrm((L, 6 * D), 0.02),
        "norm1_g": 1.0 + nrm((L, D), 0.02),
        "norm2_g": 1.0 + nrm((L, D), 0.02),
        "w_in": nrm((L, D, N_IN), D ** -0.5),
        "shift_mu": jax.random.uniform(next(ks), (L, RW_COLS), jnp.float32),
        "conv_w": nrm((L, 3, D_A), 3 ** -0.5),
        "w_up": nrm((L, 2, R_W, D_B), 0.1 * R_W ** -0.5),
        "w0": jax.random.uniform(next(ks), (L, 2, D_B), jnp.float32, -6.5, -1.5),
        "a_up": nrm((L, 2, R_A, D_B), 0.5 * R_A ** -0.5),
        "a0": nrm((L, 2, D_B), 0.1),
        "g_up": nrm((L, R_G, D_B), R_G ** -0.5),
        "k_k": 0.85 + nrm((L, D_B), 0.02),
        "k_a": 1.0 + nrm((L, D_B), 0.02),
        "r_k": nrm((L, H_B, HEAD_B), 0.1),
        "lnx_g": 1.0 + nrm((L, D_B), 0.02),
        "lnx_b": nrm((L, D_B), 0.02),
        "w_a_out": nrm((L, D_A, D), D_A ** -0.5),
        "w_b_out": nrm((L, D_B, D), D_B ** -0.5),
        "w_o": nrm((L, D, D), D ** -0.5),
        "router_g": nrm((L, D, N_GROUPS), D ** -0.5),
        "router_g_b": nrm((L, N_GROUPS), 0.01),
        "router_e": nrm((L, D, N_EXPERTS), D ** -0.5),
        "router_e_b": nrm((L, N_EXPERTS), 0.01),
        "exp_w1": nrm((L, N_EXPERTS, D, D_FF_EXPERT), D ** -0.5),
        "exp_w3": nrm((L, N_EXPERTS, D, D_FF_EXPERT), D ** -0.5),
        "exp_w2": nrm((L, N_EXPERTS, D_FF_EXPERT, D), D_FF_EXPERT ** -0.5),
        "final_g": 1.0 + nrm((D,), 0.02),
    }


def reference(x, c, ctx, c_ctx, w_mod, b_mod, norm1_g, norm2_g, w_in, shift_mu, conv_w,
              w_up, w0, a_up, a0, g_up, k_k, k_a, r_k, lnx_g, lnx_b, w_a_out, w_b_out, w_o,
              router_g, router_g_b, router_e, router_e_b, exp_w1, exp_w3, exp_w2, final_g):
    dt = x.dtype
    dm = x.shape[-1]
    xc = ctx
    s_lat = jax.nn.silu(c)
    s_ctx = jax.nn.silu(c_ctx)
    for layer in range(DEPTH):
        last = layer == DEPTH - 1
        lp = {
            "w_in": w_in[layer], "shift_mu": shift_mu[layer], "conv_w": conv_w[layer],
            "w_up": w_up[layer], "w0": w0[layer], "a_up": a_up[layer], "a0": a0[layer],
            "g_up": g_up[layer], "k_k": k_k[layer], "k_a": k_a[layer], "r_k": r_k[layer],
            "lnx_g": lnx_g[layer], "lnx_b": lnx_b[layer], "w_a_out": w_a_out[layer],
            "w_b_out": w_b_out[layer], "w_o": w_o[layer],
            "router_g": router_g[layer], "router_g_b": router_g_b[layer],
            "router_e": router_e[layer], "router_e_b": router_e_b[layer],
            "exp_w1": exp_w1[layer], "exp_w3": exp_w3[layer], "exp_w2": exp_w2[layer],
        }
        sh1, sc1, g1, sh2, sc2, g2 = [m[:, None, :] for m in jnp.split(s_lat @ w_mod[layer] + b_mod[layer], 6, axis=-1)]
        ch1, cs1, cg1, ch2, cs2, cg2 = jnp.split(s_ctx @ w_mod[layer] + b_mod[layer], 6, axis=-1)
        h_lat = _rmsnorm(x, norm1_g[layer]) * (1.0 + sc1) + sh1
        h_ctx = _rmsnorm(xc, norm1_g[layer]) * (1.0 + cs1) + ch1
        y_lat, y_ctx = _token_mixer(h_lat, h_ctx, lp, not last)
        x = x + (g1 * y_lat).astype(dt)
        h2_lat = _rmsnorm(x, norm2_g[layer]) * (1.0 + sc2) + sh2
        if last:
            y2 = _hier_moe(h2_lat.reshape(-1, dm), lp).reshape(x.shape)
            x = x + (g2 * y2).astype(dt)
        else:
            xc = xc + (cg1 * y_ctx).astype(dt)
            h2_ctx = _rmsnorm(xc, norm2_g[layer]) * (1.0 + cs2) + ch2
            nc = h2_ctx.shape[0] * h2_ctx.shape[1]
            y2 = _hier_moe(jnp.concatenate([h2_ctx.reshape(-1, dm), h2_lat.reshape(-1, dm)], axis=0), lp)
            xc = xc + (cg2 * y2[:nc].reshape(xc.shape)).astype(dt)
            x = x + (g2 * y2[nc:].reshape(x.shape)).astype(dt)
    return _rmsnorm(x, final_g)
```

```python
import functools

import jax
import jax.numpy as jnp
from jax import lax
from jax.experimental import pallas as pl
from jax.experimental.pallas import tpu as pltpu

F32 = jnp.float32
BF16 = jnp.bfloat16
HIGHEST = lax.Precision.HIGHEST

NORM_EPS = 1e-6
GN_EPS = 64e-5
KK_EPS = 1e-12

GRID_W = 64
HEAD = 64
N_GROUPS = 4
EXP_PER_GROUP = 8
N_EXPERTS = N_GROUPS * EXP_PER_GROUP
SUBLANES = 8
LANES = 128
MOD_ROWS = 24
ROUTER_COLS = LANES
SCAN_STEPS = 32
VMEM_LIMIT = 56 * 1024 * 1024


def _cparams(*sem):
    return pltpu.CompilerParams(dimension_semantics=sem, vmem_limit_bytes=VMEM_LIMIT)


def _sigmoid(x):
    return 1.0 / (1.0 + jnp.exp(-x))


def _dot_hi(a, b):
    return jnp.dot(a, b, precision=HIGHEST, preferred_element_type=F32)


def _dot_bf16(a, b):
    return jnp.dot(a.astype(BF16), b, preferred_element_type=F32)


def _rms(x, g):
    ms = jnp.mean(x * x, axis=-1, keepdims=True)
    return x * lax.rsqrt(ms + NORM_EPS) * g


def _mod_kernel(s_ref, w_ref, b_ref, o_ref):
    s = s_ref[...]
    s = s * _sigmoid(s)
    o_ref[0] = _dot_hi(s, w_ref[0]) + b_ref[0]


def _modulation(s_all, w_mod, b_mod):
    depth, d, n6 = w_mod.shape
    tn = d
    return pl.pallas_call(
        _mod_kernel,
        grid=(depth, n6 // tn),
        in_specs=[
            pl.BlockSpec((MOD_ROWS, d), lambda l, j: (0, 0)),
            pl.BlockSpec((1, d, tn), lambda l, j: (l, 0, j)),
            pl.BlockSpec((1, 1, tn), lambda l, j: (l, 0, j)),
        ],
        out_specs=pl.BlockSpec((1, MOD_ROWS, tn), lambda l, j: (l, 0, j)),
        out_shape=jax.ShapeDtypeStruct((depth, MOD_ROWS, n6), F32),
        compiler_params=_cparams("arbitrary", "arbitrary"),
        name="modulation",
    )(s_all, w_mod, b_mod.reshape(depth, 1, n6))


def _seg_of_tile(i, tm, nc_rows, seq):
    return jnp.where(i * tm < nc_rows, 0, 1 + (i * tm - nc_rows) // seq)


def _in_proj_kernel(x_ref, g_ref, sh_ref, sc_ref, wc_ref, wr_ref, wg_ref, zc_ref, zr_ref, zg_ref):
    h = _rms(x_ref[...], g_ref[...]) * (1.0 + sc_ref[0]) + sh_ref[0]
    hb = h.astype(BF16)
    zc_ref[...] = jnp.dot(hb, wc_ref[...], preferred_element_type=F32)
    zr_ref[...] = jnp.dot(hb, wr_ref[...], preferred_element_type=F32)
    zg_ref[...] = jnp.dot(hb, wg_ref[...], preferred_element_type=F32)


def _in_proj(xa, norm_g, modl, w_conv, w_rw, w_gate, *, tm, nc_rows, seq):
    n, d = xa.shape
    seg = functools.partial(_seg_of_tile, tm=tm, nc_rows=nc_rows, seq=seq)
    const = lambda shape: pl.BlockSpec(shape, lambda i: (0, 0), pipeline_mode=pl.Buffered(1))
    nc_, nr_, ng_ = w_conv.shape[1], w_rw.shape[1], w_gate.shape[1]
    return pl.pallas_call(
        _in_proj_kernel,
        grid=(n // tm,),
        in_specs=[
            pl.BlockSpec((tm, d), lambda i: (i, 0)),
            pl.BlockSpec((1, d), lambda i: (0, 0)),
            pl.BlockSpec((1, 1, d), lambda i: (seg(i), 0, 0)),
            pl.BlockSpec((1, 1, d), lambda i: (seg(i), 0, 1)),
            const((d, nc_)), const((d, nr_)), const((d, ng_)),
        ],
        out_specs=[
            pl.BlockSpec((tm, nc_), lambda i: (i, 0)),
            pl.BlockSpec((tm, nr_), lambda i: (i, 0)),
            pl.BlockSpec((tm, ng_), lambda i: (i, 0)),
        ],
        out_shape=[
            jax.ShapeDtypeStruct((n, nc_), F32),
            jax.ShapeDtypeStruct((n, nr_), F32),
            jax.ShapeDtypeStruct((n, ng_), F32),
        ],
        compiler_params=_cparams("arbitrary"),
        name="in_proj",
    )(xa, norm_g, modl, modl, w_conv, w_rw, w_gate)


def _prep_kernel(z_ref, zp_ref, zn_ref, mu_ref, kk_w_ref, ka_ref, rk_ref, w0_ref, a0_ref,
                 wup_ref, aup_ref, gup_ref, ones_ref, p_ref, q_ref, *, tm, n_ctx_tiles, tiles_per_seq):
    i = pl.program_id(0)
    is_ctx = i < n_ctx_tiles
    j = (i - n_ctx_tiles) % tiles_per_seq
    first = jnp.logical_or(is_ctx, j == 0)
    last = jnp.logical_or(is_ctx, j == tiles_per_seq - 1)
    z = z_ref[...]
    row = lax.broadcasted_iota(jnp.int32, z.shape, 0)
    prev_row = jnp.where(first, 0.0, zp_ref[SUBLANES - 1:SUBLANES, :])
    next_row = jnp.where(last, 0.0, zn_ref[0:1, :])
    z_prev = jnp.where(row == 0, prev_row, pltpu.roll(z, 1, axis=0))
    z_next = jnp.where(row == tm - 1, next_row, pltpu.roll(z, tm - 1, axis=0))
    zs = z + mu_ref[...] * (0.5 * (z_prev + z_next) - z)

    db = ones_ref.shape[0]
    r = zs[:, 0:db]
    k = zs[:, db:2 * db]
    v = zs[:, 2 * db:3 * db]
    o = 3 * db
    zw = zs[:, o:o + LANES]
    za = zs[:, o + LANES:o + 2 * LANES]
    zg = zs[:, o + 2 * LANES:o + 3 * LANES]

    kk = k * kk_w_ref[...]
    kk = kk * lax.rsqrt(_dot_hi(kk * kk, ones_ref[...]) + KK_EPS)
    u = w0_ref[...] + _dot_hi(jnp.tanh(zw), wup_ref[...])
    nu = -u
    softplus = jnp.maximum(nu, 0.0) + jnp.log(1.0 + jnp.exp(-jnp.abs(nu)))
    decay = jnp.exp(-jnp.exp(-softplus - 0.5))
    a = _sigmoid(a0_ref[...] + _dot_hi(za, aup_ref[...]))
    ka = ka_ref[...]
    a_f, a_b = a[:, :db], a[:, db:]
    kd_f = k * (1.0 + (a_f - 1.0) * ka)
    kd_b = k * (1.0 + (a_b - 1.0) * ka)
    bonus = _dot_hi(r * (kd_f + kd_b) * rk_ref[...], ones_ref[...]) * v
    gate = _dot_hi(_sigmoid(zg), gup_ref[...])

    p_ref[:, 0 * db:1 * db] = kk
    p_ref[:, 1 * db:2 * db] = r
    p_ref[:, 2 * db:3 * db] = v
    p_ref[:, 3 * db:4 * db] = decay[:, :db]
    p_ref[:, 4 * db:5 * db] = kd_f
    p_ref[:, 5 * db:6 * db] = kk * a_f
    p_ref[:, 6 * db:7 * db] = decay[:, db:]
    p_ref[:, 7 * db:8 * db] = kd_b
    p_ref[:, 8 * db:9 * db] = kk * a_b
    q_ref[:, 0:db] = bonus
    q_ref[:, db:2 * db] = gate


def _prep(zr, lp, ones_bd, *, tm, n_ctx_tiles, tiles_per_seq):
    n, nr = zr.shape
    db = ones_bd.shape[0]
    hb = tm // SUBLANES
    nblk8 = n // SUBLANES
    row = lambda a: a.reshape(1, -1)
    const = lambda a: pl.BlockSpec(a.shape, lambda i: (0,) * a.ndim)
    params = [row(lp["shift_mu"]), row(lp["k_k"]), row(lp["k_a"]), row(lp["r_k"]),
              row(lp["w0"]), row(lp["a0"]), lp["w_up_bd"], lp["a_up_bd"], lp["g_up"], ones_bd]
    kern = functools.partial(_prep_kernel, tm=tm, n_ctx_tiles=n_ctx_tiles, tiles_per_seq=tiles_per_seq)
    return pl.pallas_call(
        kern,
        grid=(n // tm,),
        in_specs=[
            pl.BlockSpec((tm, nr), lambda i: (i, 0)),
            pl.BlockSpec((SUBLANES, nr), lambda i: (jnp.maximum(i * hb - 1, 0), 0)),
            pl.BlockSpec((SUBLANES, nr), lambda i: (jnp.minimum((i + 1) * hb, nblk8 - 1), 0)),
        ] + [const(a) for a in params],
        out_specs=[
            pl.BlockSpec((tm, 9 * db), lambda i: (i, 0)),
            pl.BlockSpec((tm, 2 * db), lambda i: (i, 0)),
        ],
        out_shape=[
            jax.ShapeDtypeStruct((n, 9 * db), F32),
            jax.ShapeDtypeStruct((n, 2 * db), F32),
        ],
        compiler_params=_cparams("arbitrary"),
        name="rwkv_prep",
    )(zr, zr, zr, *params)


def _row_bcast(ref, t, k):
    return jnp.broadcast_to(ref[t, pl.ds(k, 1), :], (SUBLANES, LANES))


def _wkv_kernel(kk_ref, r_ref, v_ref, w_ref, kd_ref, b_ref, y_ref, s_scr, sa_scr, *, tc):
    d = pl.program_id(0)
    nk = s_scr.shape[0]
    nvb = s_scr.shape[1] // SUBLANES
    vrows = lambda vb: pl.ds(vb * SUBLANES, SUBLANES)

    @pl.when(pl.program_id(1) == 0)
    def _():
        s_scr[...] = jnp.zeros_like(s_scr)

    t_first = d * (tc - 1)
    acc = [jnp.zeros((SUBLANES, LANES), F32) for _ in range(nvb)]
    for k in range(nk):
        kk0 = _row_bcast(kk_ref, t_first, k)
        for vb in range(nvb):
            acc[vb] = acc[vb] + s_scr[k, vrows(vb), :] * kk0
    for vb in range(nvb):
        sa_scr[vrows(vb), :] = acc[vb]

    def step(i, carry):
        t = i + d * (tc - 1 - 2 * i)
        tn = jnp.clip(t + 1 - 2 * d, 0, tc - 1)
        sa = [sa_scr[vrows(vb), :] for vb in range(nvb)]
        vv = [v_ref[t, vrows(vb), :] for vb in range(nvb)]
        y = [jnp.zeros((SUBLANES, LANES), F32) for _ in range(nvb)]
        san = [jnp.zeros((SUBLANES, LANES), F32) for _ in range(nvb)]
        for k in range(nk):
            wk = _row_bcast(w_ref, t, k)
            bk = _row_bcast(b_ref, t, k)
            kdk = _row_bcast(kd_ref, t, k)
            rk = _row_bcast(r_ref, t, k)
            kkn = _row_bcast(kk_ref, tn, k)
            for vb in range(nvb):
                s = s_scr[k, vrows(vb), :]
                s = s * wk - sa[vb] * bk + vv[vb] * kdk
                s_scr[k, vrows(vb), :] = s
                y[vb] = y[vb] + s * rk
                san[vb] = san[vb] + s * kkn
        for vb in range(nvb):
            y_ref[t, vrows(vb), :] = y[vb]
            sa_scr[vrows(vb), :] = san[vb]
        return carry

    lax.fori_loop(0, tc, step, 0)


def _wkv_scan(p_t, *, ctx_len, tc):
    _, t_all, nk, lanes = p_t.shape
    nb = t_all // tc
    nbc = ctx_len // tc

    def blk(d, j):
        rev = jnp.where(j < nbc, nbc - 1 - j, nb - 1 - j + nbc)
        return jnp.where(d == 0, j, rev)

    shared = lambda idx: pl.BlockSpec((None, tc, nk, lanes), lambda d, j: (idx, blk(d, j), 0, 0))
    per_dir = lambda idx: pl.BlockSpec((None, tc, nk, lanes), lambda d, j: (idx + 3 * d, blk(d, j), 0, 0))
    return pl.pallas_call(
        functools.partial(_wkv_kernel, tc=tc),
        grid=(2, nb),
        in_specs=[shared(0), shared(1), shared(2), per_dir(3), per_dir(4), per_dir(5)],
        out_specs=pl.BlockSpec((None, tc, nk, lanes), lambda d, j: (d, blk(d, j), 0, 0)),
        out_shape=jax.ShapeDtypeStruct((2, t_all, nk, lanes), F32),
        scratch_shapes=[pltpu.VMEM((nk, nk, lanes), F32), pltpu.VMEM((nk, lanes), F32)],
        compiler_params=_cparams("arbitrary", "arbitrary"),
        name="wkv_scan",
    )(p_t, p_t, p_t, p_t, p_t, p_t)


def _mix_out_kernel(x_ref, zc_ref, zcp_ref, zcn_ref, y_ref, q_ref, zg_ref,
                    g1_ref, sh2_ref, sc2_ref, n2_ref, cw_ref, lng_ref, lnb_ref, ones_ref,
                    wa_ref, wb_ref, wo_ref, wr_ref, br_ref,
                    xo_ref, h2_ref, lg_ref, *, tm, n_ctx_tiles, tiles_per_seq):
    i = pl.program_id(0)
    is_ctx = i < n_ctx_tiles
    j = (i - n_ctx_tiles) % tiles_per_seq
    first = jnp.logical_or(is_ctx, j == 0)
    last = jnp.logical_or(is_ctx, j == tiles_per_seq - 1)
    da = ones_ref.shape[0]
    half = da // 2

    zc = zc_ref[...]
    bg, u = zc[:, 0:da], zc[:, da:2 * da] * zc[:, 2 * da:3 * da]
    row = lax.broadcasted_iota(jnp.int32, u.shape, 0)
    period = jnp.where(is_ctx, tm, GRID_W)
    pos = jnp.bitwise_and(row, period - 1)
    u_m1 = jnp.where(pos == 0, 0.0, pltpu.roll(u, 1, axis=0))
    u_p1 = jnp.where(pos == period - 1, 0.0, pltpu.roll(u, tm - 1, axis=0))
    uh = u[:, half:]
    hp = zcp_ref[:, da + half:2 * da] * zcp_ref[:, 2 * da + half:3 * da]
    hn = zcn_ref[:, da + half:2 * da] * zcn_ref[:, 2 * da + half:3 * da]
    hp = jnp.where(first, 0.0, hp)
    hn = jnp.where(last, 0.0, hn)
    uv_m = jnp.concatenate([hp, uh[:tm - GRID_W]], axis=0)
    uv_p = jnp.concatenate([uh[GRID_W:], hn], axis=0)
    um_hi = jnp.where(is_ctx, u_m1[:, half:], uv_m)
    up_hi = jnp.where(is_ctx, u_p1[:, half:], uv_p)
    cw = cw_ref[...]
    conv_lo = u_m1[:, :half] * cw[0:1, :half] + u[:, :half] * cw[1:2, :half] + u_p1[:, :half] * cw[2:3, :half]
    conv_hi = um_hi * cw[0:1, half:] + uh * cw[1:2, half:] + up_hi * cw[2:3, half:]
    ya = (_dot_bf16(bg[:, :half] * conv_lo, wa_ref[0:half, :])
          + _dot_bf16(bg[:, half:] * conv_hi, wa_ref[half:, :]))

    y = y_ref[...]
    inv_n = 1.0 / HEAD
    mean = _dot_hi(y, ones_ref[...]) * inv_n
    yc = y - mean
    var = _dot_hi(yc * yc, ones_ref[...]) * inv_n
    yn = yc * lax.rsqrt(var + GN_EPS) * lng_ref[...] + lnb_ref[...]
    db = y.shape[1]
    yb = _dot_bf16((yn + q_ref[:, 0:db]) * q_ref[:, db:2 * db], wb_ref[...])

    d = x_ref.shape[1]
    merged = _sigmoid(zg_ref[:, 0:d]) * ya + _sigmoid(zg_ref[:, d:2 * d]) * yb
    x_new = x_ref[...] + g1_ref[0] * _dot_bf16(merged, wo_ref[...])
    xo_ref[...] = x_new

    h2 = _rms(x_new, n2_ref[...]) * (1.0 + sc2_ref[0]) + sh2_ref[0]
    h2_ref[...] = h2.astype(BF16)
    lg_ref[...] = _dot_hi(h2, wr_ref[...]) + br_ref[...]


def _mix_out(xa, zc, y, q, zg, modl, lp, ones_bd, *, tm, n_ctx_tiles, tiles_per_seq, nc_rows, seq):
    n, d = xa.shape
    da3 = zc.shape[1]
    db = y.shape[1]
    hb = tm // GRID_W
    nblk = n // GRID_W
    seg = functools.partial(_seg_of_tile, tm=tm, nc_rows=nc_rows, seq=seq)
    mod = lambda c: pl.BlockSpec((1, 1, d), lambda i: (seg(i), 0, c))
    const = lambda a: pl.BlockSpec(a.shape, lambda i: (0,) * a.ndim)
    tile = lambda w: pl.BlockSpec((tm, w), lambda i: (i, 0))
    params = [lp["norm2_g"].reshape(1, d), lp["conv_w"], lp["lnx_g"].reshape(1, db), lp["lnx_b"].reshape(1, db),
              ones_bd, lp["w_a_out"], lp["w_b_out"], lp["w_o"], lp["router_w"], lp["router_b"]]
    kern = functools.partial(_mix_out_kernel, tm=tm, n_ctx_tiles=n_ctx_tiles, tiles_per_seq=tiles_per_seq)
    return pl.pallas_call(
        kern,
        grid=(n // tm,),
        in_specs=[
            tile(d), tile(da3),
            pl.BlockSpec((GRID_W, da3), lambda i: (jnp.maximum(i * hb - 1, 0), 0)),
            pl.BlockSpec((GRID_W, da3), lambda i: (jnp.minimum((i + 1) * hb, nblk - 1), 0)),
            tile(db), tile(2 * db), tile(2 * d),
            mod(2), mod(3), mod(4),
        ] + [const(a) for a in params],
        out_specs=[tile(d), tile(d), tile(ROUTER_COLS)],
        out_shape=[
            jax.ShapeDtypeStruct((n, d), F32),
            jax.ShapeDtypeStruct((n, d), BF16),
            jax.ShapeDtypeStruct((n, ROUTER_COLS), F32),
        ],
        compiler_params=_cparams("arbitrary"),
        name="mix_out",
    )(xa, zc, zc, zc, y, q, zg, modl, modl, modl, *params)


def _moe_kernel(h_ref, gt_ref, w1_ref, w3_ref, w2_ref, x_ref, g2_ref, o_ref, acc_ref):
    e = pl.program_id(1)

    @pl.when(e == 0)
    def _():
        acc_ref[...] = jnp.zeros_like(acc_ref)

    h = h_ref[...]
    a = jnp.dot(h, w1_ref[0], preferred_element_type=F32)
    b = jnp.dot(h, w3_ref[0], preferred_element_type=F32)
    hid = a * _sigmoid(a) * b
    ye = _dot_bf16(hid, w2_ref[0])
    gates = gt_ref[...]
    lane = lax.broadcasted_iota(jnp.int32, gates.shape, 1)
    ge = jnp.sum(jnp.where(lane == e, gates, 0.0), axis=1, keepdims=True)
    acc_ref[...] += ye * ge

    @pl.when(e == pl.num_programs(1) - 1)
    def _():
        o_ref[...] = x_ref[...] + g2_ref[0] * acc_ref[...]


def _moe(h2, gates, xa, modl, w1, w3, w2, *, tm, nc_rows, seq):
    n, d = xa.shape
    ne, _, f = w1.shape
    seg = functools.partial(_seg_of_tile, tm=tm, nc_rows=nc_rows, seq=seq)
    return pl.pallas_call(
        _moe_kernel,
        grid=(n // tm, ne),
        in_specs=[
            pl.BlockSpec((tm, d), lambda i, e: (i, 0)),
            pl.BlockSpec((tm, LANES), lambda i, e: (i, 0)),
            pl.BlockSpec((1, d, f), lambda i, e: (e, 0, 0)),
            pl.BlockSpec((1, d, f), lambda i, e: (e, 0, 0)),
            pl.BlockSpec((1, f, d), lambda i, e: (e, 0, 0)),
            pl.BlockSpec((tm, d), lambda i, e: (i, 0)),
            pl.BlockSpec((1, 1, d), lambda i, e: (seg(i), 0, 5)),
        ],
        out_specs=pl.BlockSpec((tm, d), lambda i, e: (i, 0)),
        out_shape=jax.ShapeDtypeStruct((n, d), F32),
        scratch_shapes=[pltpu.VMEM((tm, d), F32)],
        compiler_params=_cparams("arbitrary", "arbitrary"),
        name="moe_experts",
    )(h2, gates, w1, w3, w2, xa, modl)


def _route(logits):
    lg = logits[:, :N_GROUPS]
    le = logits[:, N_GROUPS:N_GROUPS + N_EXPERTS].reshape(-1, N_GROUPS, EXP_PER_GROUP)
    p_g = jax.nn.softmax(lg, axis=-1)
    g_sel = jnp.argmax(lg, axis=-1).astype(jnp.int32)
    le = jnp.take_along_axis(le, g_sel[:, None, None], axis=1)[:, 0]
    top_p, top_i = lax.top_k(jax.nn.softmax(le, axis=-1), 2)
    gate = jnp.max(p_g, axis=-1, keepdims=True) * top_p / jnp.sum(top_p, axis=-1, keepdims=True)
    expert = g_sel[:, None] * EXP_PER_GROUP + top_i
    cols = jnp.arange(LANES, dtype=jnp.int32)[None, None, :]
    return jnp.sum(jnp.where(expert[:, :, None] == cols, gate[:, :, None], 0.0), axis=1)


def _final_norm_kernel(x_ref, g_ref, o_ref):
    o_ref[...] = _rms(x_ref[...], g_ref[...])


def _final_norm(xa, g, *, tm, row_off, n_out):
    d = xa.shape[1]
    off = row_off // tm
    return pl.pallas_call(
        _final_norm_kernel,
        grid=(n_out // tm,),
        in_specs=[pl.BlockSpec((tm, d), lambda i: (i + off, 0)), pl.BlockSpec((1, d), lambda i: (0, 0))],
        out_specs=pl.BlockSpec((tm, d), lambda i: (i, 0)),
        out_shape=jax.ShapeDtypeStruct((n_out, d), F32),
        compiler_params=_cparams("arbitrary"),
        name="final_norm",
    )(xa, g.reshape(1, d))


def _block_diag2(m):
    z = jnp.zeros_like(m[0])
    return jnp.concatenate([jnp.concatenate([m[0], z], axis=1), jnp.concatenate([z, m[1]], axis=1)], axis=0)


def kernel(x, c, ctx, c_ctx, w_mod, b_mod, norm1_g, norm2_g, w_in, shift_mu, conv_w, w_up, w0, a_up, a0, g_up, k_k, k_a, r_k, lnx_g, lnx_b, w_a_out, w_b_out, w_o, router_g, router_g_b, router_e, router_e_b, exp_w1, exp_w3, exp_w2, final_g):
    bsz, seq, d = x.shape
    ctx_len = ctx.shape[1]
    depth = w_mod.shape[0]
    da = w_a_out.shape[1]
    db = w_b_out.shape[1]
    heads = db // HEAD
    assert bsz * heads == LANES and da == db and seq % ctx_len == 0 and ctx_len % GRID_W == 0
    assert bsz + 1 <= MOD_ROWS
    nc_rows = bsz * ctx_len
    n_all = nc_rows + bsz * seq
    t_all = ctx_len + seq
    tm = ctx_len
    tm_mm = min(512, seq)
    tc = min(SCAN_STEPS, ctx_len)
    tiles = dict(tm=tm, n_ctx_tiles=bsz, tiles_per_seq=seq // tm)
    segs = dict(nc_rows=nc_rows, seq=seq)

    s_all = jnp.zeros((MOD_ROWS, d), F32).at[0].set(c_ctx).at[1:1 + bsz].set(c)
    mod = _modulation(s_all, w_mod, b_mod)
    xa = jnp.concatenate([ctx.reshape(nc_rows, d), x.reshape(bsz * seq, d)], axis=0)
    ones_bd = jnp.kron(jnp.eye(heads, dtype=F32), jnp.ones((HEAD, HEAD), F32))
    rw0 = 3 * da
    rw1 = rw0 + 3 * db + 3 * LANES

    for l in range(depth):
        modl = mod[l].reshape(MOD_ROWS, 1, 6 * d)
        w_in_b = w_in[l].astype(BF16)
        router_w = jnp.zeros((d, ROUTER_COLS), F32)
        router_w = router_w.at[:, :N_GROUPS].set(router_g[l]).at[:, N_GROUPS:N_GROUPS + N_EXPERTS].set(router_e[l])
        router_b = jnp.zeros((1, ROUTER_COLS), F32)
        router_b = router_b.at[0, :N_GROUPS].set(router_g_b[l]).at[0, N_GROUPS:N_GROUPS + N_EXPERTS].set(router_e_b[l])
        lp = {
            "shift_mu": shift_mu[l], "k_k": k_k[l], "k_a": k_a[l], "r_k": r_k[l],
            "w0": w0[l], "a0": a0[l], "w_up_bd": _block_diag2(w_up[l]), "a_up_bd": _block_diag2(a_up[l]),
            "g_up": g_up[l], "norm2_g": norm2_g[l], "conv_w": conv_w[l], "lnx_g": lnx_g[l], "lnx_b": lnx_b[l],
            "w_a_out": w_a_out[l].astype(BF16), "w_b_out": w_b_out[l].astype(BF16), "w_o": w_o[l].astype(BF16),
            "router_w": router_w, "router_b": router_b,
        }
        zc, zr, zg = _in_proj(xa, norm1_g[l].reshape(1, d), modl, w_in_b[:, :rw0], w_in_b[:, rw0:rw1],
                              w_in_b[:, rw1:], tm=tm_mm, **segs)
        p, q = _prep(zr, lp, ones_bd, **tiles)
        p_c = p[:nc_rows].reshape(bsz, ctx_len, 9, heads, HEAD)
        p_l = p[nc_rows:].reshape(bsz, seq, 9, heads, HEAD)
        p_t = jnp.concatenate([p_c, p_l], axis=1).transpose(2, 1, 4, 0, 3).reshape(9, t_all, HEAD, LANES)
        y_t = _wkv_scan(p_t, ctx_len=ctx_len, tc=tc)
        y_bt = (y_t[0] + y_t[1]).reshape(t_all, HEAD, bsz, heads).transpose(2, 0, 3, 1)
        y = jnp.concatenate([y_bt[:, :ctx_len].reshape(nc_rows, db), y_bt[:, ctx_len:].reshape(bsz * seq, db)], axis=0)
        xa, h2, logits = _mix_out(xa, zc, y, q, zg, modl, lp, ones_bd, **tiles, **segs)
        gates = _route(logits)
        xa = _moe(h2, gates, xa, modl, exp_w1[l].astype(BF16), exp_w3[l].astype(BF16), exp_w2[l].astype(BF16),
                  tm=tm_mm, **segs)

    out = _final_norm(xa, final_g, tm=tm, row_off=nc_rows, n_out=bsz * seq)
    return out.reshape(bsz, seq, d)
```

```python
import functools

import jax
import jax.numpy as jnp
from jax import lax
from jax.experimental import pallas as pl
from jax.experimental.pallas import tpu as pltpu

F32 = jnp.float32
BF16 = jnp.bfloat16
HIGHEST = lax.Precision.HIGHEST

NORM_EPS = 1e-6
GN_EPS = 64e-5
KK_EPS = 1e-12

GRID_W = 64
HEAD = 64
N_GROUPS = 4
EXP_PER_GROUP = 8
N_EXPERTS = N_GROUPS * EXP_PER_GROUP
SUBLANES = 8
LANES = 128
MOD_ROWS = 24
ROUTER_COLS = LANES
SCAN_STEPS = 32
MOE_BLOCK = 256
VMEM_LIMIT = 56 * 1024 * 1024


def _cparams(*sem):
    return pltpu.CompilerParams(dimension_semantics=sem, vmem_limit_bytes=VMEM_LIMIT)


def _sigmoid(x):
    return 1.0 / (1.0 + jnp.exp(-x))


def _dot_hi(a, b):
    return jnp.dot(a, b, precision=HIGHEST, preferred_element_type=F32)


def _dot_bf16(a, b):
    return jnp.dot(a.astype(BF16), b, preferred_element_type=F32)


def _rms(x, g):
    ms = jnp.mean(x * x, axis=-1, keepdims=True)
    return x * lax.rsqrt(ms + NORM_EPS) * g


def _mod_kernel(s_ref, w_ref, b_ref, o_ref):
    s = s_ref[...]
    s = s * _sigmoid(s)
    o_ref[0] = _dot_hi(s, w_ref[0]) + b_ref[0]


def _modulation(s_all, w_mod, b_mod):
    depth, d, n6 = w_mod.shape
    tn = d
    return pl.pallas_call(
        _mod_kernel,
        grid=(depth, n6 // tn),
        in_specs=[
            pl.BlockSpec((MOD_ROWS, d), lambda l, j: (0, 0)),
            pl.BlockSpec((1, d, tn), lambda l, j: (l, 0, j)),
            pl.BlockSpec((1, 1, tn), lambda l, j: (l, 0, j)),
        ],
        out_specs=pl.BlockSpec((1, MOD_ROWS, tn), lambda l, j: (l, 0, j)),
        out_shape=jax.ShapeDtypeStruct((depth, MOD_ROWS, n6), F32),
        compiler_params=_cparams("arbitrary", "arbitrary"),
        name="modulation",
    )(s_all, w_mod, b_mod.reshape(depth, 1, n6))


def _seg_of_tile(i, tm, nc_rows, seq):
    return jnp.where(i * tm < nc_rows, 0, 1 + (i * tm - nc_rows) // seq)


def _in_proj_kernel(x_ref, g_ref, sh_ref, sc_ref, wc_ref, wr_ref, wg_ref, zc_ref, zr_ref, zg_ref):
    h = _rms(x_ref[...], g_ref[...]) * (1.0 + sc_ref[0]) + sh_ref[0]
    hb = h.astype(BF16)
    zc_ref[...] = jnp.dot(hb, wc_ref[...], preferred_element_type=F32)
    zr_ref[...] = jnp.dot(hb, wr_ref[...], preferred_element_type=F32)
    zg_ref[...] = jnp.dot(hb, wg_ref[...], preferred_element_type=F32)


def _in_proj(xa, norm_g, modl, w_conv, w_rw, w_gate, *, tm, nc_rows, seq):
    n, d = xa.shape
    seg = functools.partial(_seg_of_tile, tm=tm, nc_rows=nc_rows, seq=seq)
    const = lambda shape: pl.BlockSpec(shape, lambda i: (0, 0), pipeline_mode=pl.Buffered(1))
    nc_, nr_, ng_ = w_conv.shape[1], w_rw.shape[1], w_gate.shape[1]
    return pl.pallas_call(
        _in_proj_kernel,
        grid=(n // tm,),
        in_specs=[
            pl.BlockSpec((tm, d), lambda i: (i, 0)),
            pl.BlockSpec((1, d), lambda i: (0, 0)),
            pl.BlockSpec((1, 1, d), lambda i: (seg(i), 0, 0)),
            pl.BlockSpec((1, 1, d), lambda i: (seg(i), 0, 1)),
            const((d, nc_)), const((d, nr_)), const((d, ng_)),
        ],
        out_specs=[
            pl.BlockSpec((tm, nc_), lambda i: (i, 0)),
            pl.BlockSpec((tm, nr_), lambda i: (i, 0)),
            pl.BlockSpec((tm, ng_), lambda i: (i, 0)),
        ],
        out_shape=[
            jax.ShapeDtypeStruct((n, nc_), F32),
            jax.ShapeDtypeStruct((n, nr_), F32),
            jax.ShapeDtypeStruct((n, ng_), F32),
        ],
        compiler_params=_cparams("arbitrary"),
        name="in_proj",
    )(xa, norm_g, modl, modl, w_conv, w_rw, w_gate)


def _prep_kernel(z_ref, zp_ref, zn_ref, mu_ref, kk_w_ref, ka_ref, rk_ref, w0_ref, a0_ref,
                 wup_ref, aup_ref, gup_ref, ones_ref, p_ref, q_ref, *, tm, n_ctx_tiles, tiles_per_seq):
    i = pl.program_id(0)
    is_ctx = i < n_ctx_tiles
    j = (i - n_ctx_tiles) % tiles_per_seq
    first = jnp.logical_or(is_ctx, j == 0)
    last = jnp.logical_or(is_ctx, j == tiles_per_seq - 1)
    z = z_ref[...]
    row = lax.broadcasted_iota(jnp.int32, z.shape, 0)
    prev_row = jnp.where(first, 0.0, zp_ref[SUBLANES - 1:SUBLANES, :])
    next_row = jnp.where(last, 0.0, zn_ref[0:1, :])
    z_prev = jnp.where(row == 0, prev_row, pltpu.roll(z, 1, axis=0))
    z_next = jnp.where(row == tm - 1, next_row, pltpu.roll(z, tm - 1, axis=0))
    zs = z + mu_ref[...] * (0.5 * (z_prev + z_next) - z)

    db = ones_ref.shape[0]
    r = zs[:, 0:db]
    k = zs[:, db:2 * db]
    v = zs[:, 2 * db:3 * db]
    o = 3 * db
    zw = zs[:, o:o + LANES]
    za = zs[:, o + LANES:o + 2 * LANES]
    zg = zs[:, o + 2 * LANES:o + 3 * LANES]

    kk = k * kk_w_ref[...]
    kk = kk * lax.rsqrt(_dot_hi(kk * kk, ones_ref[...]) + KK_EPS)
    u = w0_ref[...] + _dot_hi(jnp.tanh(zw), wup_ref[...])
    nu = -u
    softplus = jnp.maximum(nu, 0.0) + jnp.log(1.0 + jnp.exp(-jnp.abs(nu)))
    decay = jnp.exp(-jnp.exp(-softplus - 0.5))
    a = _sigmoid(a0_ref[...] + _dot_hi(za, aup_ref[...]))
    ka = ka_ref[...]
    a_f, a_b = a[:, :db], a[:, db:]
    kd_f = k * (1.0 + (a_f - 1.0) * ka)
    kd_b = k * (1.0 + (a_b - 1.0) * ka)
    bonus = _dot_hi(r * (kd_f + kd_b) * rk_ref[...], ones_ref[...]) * v
    gate = _dot_hi(_sigmoid(zg), gup_ref[...])

    p_ref[:, 0 * db:1 * db] = kk
    p_ref[:, 1 * db:2 * db] = r
    p_ref[:, 2 * db:3 * db] = v
    p_ref[:, 3 * db:4 * db] = decay[:, :db]
    p_ref[:, 4 * db:5 * db] = kd_f
    p_ref[:, 5 * db:6 * db] = kk * a_f
    p_ref[:, 6 * db:7 * db] = decay[:, db:]
    p_ref[:, 7 * db:8 * db] = kd_b
    p_ref[:, 8 * db:9 * db] = kk * a_b
    q_ref[:, 0:db] = bonus
    q_ref[:, db:2 * db] = gate


def _prep(zr, lp, ones_bd, *, tm, n_ctx_tiles, tiles_per_seq):
    n, nr = zr.shape
    db = ones_bd.shape[0]
    hb = tm // SUBLANES
    nblk8 = n // SUBLANES
    row = lambda a: a.reshape(1, -1)
    const = lambda a: pl.BlockSpec(a.shape, lambda i: (0,) * a.ndim)
    params = [row(lp["shift_mu"]), row(lp["k_k"]), row(lp["k_a"]), row(lp["r_k"]),
              row(lp["w0"]), row(lp["a0"]), lp["w_up_bd"], lp["a_up_bd"], lp["g_up"], ones_bd]
    kern = functools.partial(_prep_kernel, tm=tm, n_ctx_tiles=n_ctx_tiles, tiles_per_seq=tiles_per_seq)
    return pl.pallas_call(
        kern,
        grid=(n // tm,),
        in_specs=[
            pl.BlockSpec((tm, nr), lambda i: (i, 0)),
            pl.BlockSpec((SUBLANES, nr), lambda i: (jnp.maximum(i * hb - 1, 0), 0)),
            pl.BlockSpec((SUBLANES, nr), lambda i: (jnp.minimum((i + 1) * hb, nblk8 - 1), 0)),
        ] + [const(a) for a in params],
        out_specs=[
            pl.BlockSpec((tm, 9 * db), lambda i: (i, 0)),
            pl.BlockSpec((tm, 2 * db), lambda i: (i, 0)),
        ],
        out_shape=[
            jax.ShapeDtypeStruct((n, 9 * db), F32),
            jax.ShapeDtypeStruct((n, 2 * db), F32),
        ],
        compiler_params=_cparams("arbitrary"),
        name="rwkv_prep",
    )(zr, zr, zr, *params)


def _row_bcast(ref, t, k):
    return jnp.broadcast_to(ref[t, pl.ds(k, 1), :], (SUBLANES, LANES))


def _wkv_kernel(kk_ref, r_ref, v_ref, w_ref, kd_ref, b_ref, y_ref, s_scr, sa_scr, *, tc):
    d = pl.program_id(0)
    nk = s_scr.shape[0]
    nvb = s_scr.shape[1] // SUBLANES
    vrows = lambda vb: pl.ds(vb * SUBLANES, SUBLANES)

    @pl.when(pl.program_id(1) == 0)
    def _():
        s_scr[...] = jnp.zeros_like(s_scr)

    t_first = d * (tc - 1)
    acc = [jnp.zeros((SUBLANES, LANES), F32) for _ in range(nvb)]
    for k in range(nk):
        kk0 = _row_bcast(kk_ref, t_first, k)
        for vb in range(nvb):
            acc[vb] = acc[vb] + s_scr[k, vrows(vb), :] * kk0
    for vb in range(nvb):
        sa_scr[vrows(vb), :] = acc[vb]

    def step(i, carry):
        t = i + d * (tc - 1 - 2 * i)
        tn = jnp.clip(t + 1 - 2 * d, 0, tc - 1)
        sa = [sa_scr[vrows(vb), :] for vb in range(nvb)]
        vv = [v_ref[t, vrows(vb), :] for vb in range(nvb)]
        y = [jnp.zeros((SUBLANES, LANES), F32) for _ in range(nvb)]
        san = [jnp.zeros((SUBLANES, LANES), F32) for _ in range(nvb)]
        for k in range(nk):
            wk = _row_bcast(w_ref, t, k)
            bk = _row_bcast(b_ref, t, k)
            kdk = _row_bcast(kd_ref, t, k)
            rk = _row_bcast(r_ref, t, k)
            kkn = _row_bcast(kk_ref, tn, k)
            for vb in range(nvb):
                s = s_scr[k, vrows(vb), :]
                s = s * wk - sa[vb] * bk + vv[vb] * kdk
                s_scr[k, vrows(vb), :] = s
                y[vb] = y[vb] + s * rk
                san[vb] = san[vb] + s * kkn
        for vb in range(nvb):
            y_ref[t, vrows(vb), :] = y[vb]
            sa_scr[vrows(vb), :] = san[vb]
        return carry

    lax.fori_loop(0, tc, step, 0)


def _wkv_scan(p_t, *, ctx_len, tc):
    _, t_all, nk, lanes = p_t.shape
    nb = t_all // tc
    nbc = ctx_len // tc

    def blk(d, j):
        rev = jnp.where(j < nbc, nbc - 1 - j, nb - 1 - j + nbc)
        return jnp.where(d == 0, j, rev)

    shared = lambda idx: pl.BlockSpec((None, tc, nk, lanes), lambda d, j: (idx, blk(d, j), 0, 0))
    per_dir = lambda idx: pl.BlockSpec((None, tc, nk, lanes), lambda d, j: (idx + 3 * d, blk(d, j), 0, 0))
    return pl.pallas_call(
        functools.partial(_wkv_kernel, tc=tc),
        grid=(2, nb),
        in_specs=[shared(0), shared(1), shared(2), per_dir(3), per_dir(4), per_dir(5)],
        out_specs=pl.BlockSpec((None, tc, nk, lanes), lambda d, j: (d, blk(d, j), 0, 0)),
        out_shape=jax.ShapeDtypeStruct((2, t_all, nk, lanes), F32),
        scratch_shapes=[pltpu.VMEM((nk, nk, lanes), F32), pltpu.VMEM((nk, lanes), F32)],
        compiler_params=_cparams("arbitrary", "arbitrary"),
        name="wkv_scan",
    )(p_t, p_t, p_t, p_t, p_t, p_t)


def _mix_out_kernel(x_ref, zc_ref, zcp_ref, zcn_ref, y_ref, q_ref, zg_ref,
                    g1_ref, sh2_ref, sc2_ref, n2_ref, cw_ref, lng_ref, lnb_ref, ones_ref,
                    wa_ref, wb_ref, wo_ref, wr_ref, br_ref,
                    xo_ref, h2_ref, lg_ref, *, tm, n_ctx_tiles, tiles_per_seq):
    i = pl.program_id(0)
    is_ctx = i < n_ctx_tiles
    j = (i - n_ctx_tiles) % tiles_per_seq
    first = jnp.logical_or(is_ctx, j == 0)
    last = jnp.logical_or(is_ctx, j == tiles_per_seq - 1)
    da = ones_ref.shape[0]
    half = da // 2

    zc = zc_ref[...]
    bg, u = zc[:, 0:da], zc[:, da:2 * da] * zc[:, 2 * da:3 * da]
    row = lax.broadcasted_iota(jnp.int32, u.shape, 0)
    period = jnp.where(is_ctx, tm, GRID_W)
    pos = jnp.bitwise_and(row, period - 1)
    u_m1 = jnp.where(pos == 0, 0.0, pltpu.roll(u, 1, axis=0))
    u_p1 = jnp.where(pos == period - 1, 0.0, pltpu.roll(u, tm - 1, axis=0))
    uh = u[:, half:]
    hp = zcp_ref[:, da + half:2 * da] * zcp_ref[:, 2 * da + half:3 * da]
    hn = zcn_ref[:, da + half:2 * da] * zcn_ref[:, 2 * da + half:3 * da]
    hp = jnp.where(first, 0.0, hp)
    hn = jnp.where(last, 0.0, hn)
    uv_m = jnp.concatenate([hp, uh[:tm - GRID_W]], axis=0)
    uv_p = jnp.concatenate([uh[GRID_W:], hn], axis=0)
    um_hi = jnp.where(is_ctx, u_m1[:, half:], uv_m)
    up_hi = jnp.where(is_ctx, u_p1[:, half:], uv_p)
    cw = cw_ref[...]
    conv_lo = u_m1[:, :half] * cw[0:1, :half] + u[:, :half] * cw[1:2, :half] + u_p1[:, :half] * cw[2:3, :half]
    conv_hi = um_hi * cw[0:1, half:] + uh * cw[1:2, half:] + up_hi * cw[2:3, half:]
    ya = (_dot_bf16(bg[:, :half] * conv_lo, wa_ref[0:half, :])
          + _dot_bf16(bg[:, half:] * conv_hi, wa_ref[half:, :]))

    y = y_ref[...]
    inv_n = 1.0 / HEAD
    mean = _dot_hi(y, ones_ref[...]) * inv_n
    yc = y - mean
    var = _dot_hi(yc * yc, ones_ref[...]) * inv_n
    yn = yc * lax.rsqrt(var + GN_EPS) * lng_ref[...] + lnb_ref[...]
    db = y.shape[1]
    yb = _dot_bf16((yn + q_ref[:, 0:db]) * q_ref[:, db:2 * db], wb_ref[...])

    d = x_ref.shape[1]
    merged = _sigmoid(zg_ref[:, 0:d]) * ya + _sigmoid(zg_ref[:, d:2 * d]) * yb
    x_new = x_ref[...] + g1_ref[0] * _dot_bf16(merged, wo_ref[...])
    xo_ref[...] = x_new

    h2 = _rms(x_new, n2_ref[...]) * (1.0 + sc2_ref[0]) + sh2_ref[0]
    h2_ref[...] = h2
    lg_ref[...] = _dot_hi(h2, wr_ref[...]) + br_ref[...]


def _mix_out(xa, zc, y, q, zg, modl, lp, ones_bd, *, tm, n_ctx_tiles, tiles_per_seq, nc_rows, seq):
    n, d = xa.shape
    da3 = zc.shape[1]
    db = y.shape[1]
    hb = tm // GRID_W
    nblk = n // GRID_W
    seg = functools.partial(_seg_of_tile, tm=tm, nc_rows=nc_rows, seq=seq)
    mod = lambda c: pl.BlockSpec((1, 1, d), lambda i: (seg(i), 0, c))
    const = lambda a: pl.BlockSpec(a.shape, lambda i: (0,) * a.ndim)
    tile = lambda w: pl.BlockSpec((tm, w), lambda i: (i, 0))
    params = [lp["norm2_g"].reshape(1, d), lp["conv_w"], lp["lnx_g"].reshape(1, db), lp["lnx_b"].reshape(1, db),
              ones_bd, lp["w_a_out"], lp["w_b_out"], lp["w_o"], lp["router_w"], lp["router_b"]]
    kern = functools.partial(_mix_out_kernel, tm=tm, n_ctx_tiles=n_ctx_tiles, tiles_per_seq=tiles_per_seq)
    return pl.pallas_call(
        kern,
        grid=(n // tm,),
        in_specs=[
            tile(d), tile(da3),
            pl.BlockSpec((GRID_W, da3), lambda i: (jnp.maximum(i * hb - 1, 0), 0)),
            pl.BlockSpec((GRID_W, da3), lambda i: (jnp.minimum((i + 1) * hb, nblk - 1), 0)),
            tile(db), tile(2 * db), tile(2 * d),
            mod(2), mod(3), mod(4),
        ] + [const(a) for a in params],
        out_specs=[tile(d), tile(d), tile(ROUTER_COLS)],
        out_shape=[
            jax.ShapeDtypeStruct((n, d), F32),
            jax.ShapeDtypeStruct((n, d), F32),
            jax.ShapeDtypeStruct((n, ROUTER_COLS), F32),
        ],
        compiler_params=_cparams("arbitrary"),
        name="mix_out",
    )(xa, zc, zc, zc, y, q, zg, modl, modl, modl, *params)


def _start_row_gather(src_hbm, idx_ref, base, stride, buf, slot, sem, n_rows):
    def body(r, carry):
        row = idx_ref[base + r * stride]
        pltpu.make_async_copy(src_hbm.at[pl.ds(row, 1)], buf.at[slot, pl.ds(r, 1)], sem.at[slot]).start()
        return carry
    lax.fori_loop(0, n_rows, body, 0, unroll=8)


def _wait_row_gather(src_hbm, buf, slot, sem, n_rows):
    def body(r, carry):
        pltpu.make_async_copy(src_hbm.at[pl.ds(0, 1)], buf.at[slot, pl.ds(r, 1)], sem.at[slot]).wait()
        return carry
    lax.fori_loop(0, n_rows, body, 0, unroll=8)


def _moe_expert_kernel(bexp_ref, nused_ref, tok_ref, h_hbm, gate_ref, w1_ref, w3_ref, w2_ref, o_ref, buf, sem):
    del bexp_ref
    i = pl.program_id(0)
    n_used = nused_ref[0]
    blk = o_ref.shape[0]
    slot = lax.rem(i, 2)

    @pl.when(jnp.logical_and(i == 0, n_used > 0))
    def _():
        _start_row_gather(h_hbm, tok_ref, 0, 1, buf, 0, sem, blk)

    @pl.when(i + 1 < n_used)
    def _():
        _start_row_gather(h_hbm, tok_ref, (i + 1) * blk, 1, buf, 1 - slot, sem, blk)

    @pl.when(i < n_used)
    def _():
        _wait_row_gather(h_hbm, buf, slot, sem, blk)
        h = buf[slot].astype(BF16)
        a = jnp.dot(h, w1_ref[0], preferred_element_type=F32)
        b = jnp.dot(h, w3_ref[0], preferred_element_type=F32)
        hid = a * _sigmoid(a) * b
        o_ref[...] = _dot_bf16(hid, w2_ref[0]) * gate_ref[...]

    @pl.when(i >= n_used)
    def _():
        o_ref[...] = jnp.zeros_like(o_ref)


def _moe_experts(h2, disp, w1, w3, w2):
    n, d = h2.shape
    ne, _, f = w1.shape
    blk = MOE_BLOCK
    n_slots = disp["slot_tok"].shape[0]
    wspec = lambda shape: pl.BlockSpec((1,) + shape, lambda i, bexp, nused, tok: (bexp[i], 0, 0))
    return pl.pallas_call(
        _moe_expert_kernel,
        grid_spec=pltpu.PrefetchScalarGridSpec(
            num_scalar_prefetch=3,
            grid=(n_slots // blk,),
            in_specs=[
                pl.BlockSpec(memory_space=pl.ANY),
                pl.BlockSpec((blk, 1), lambda i, bexp, nused, tok: (i, 0)),
                wspec((d, f)), wspec((d, f)), wspec((f, d)),
            ],
            out_specs=pl.BlockSpec((blk, d), lambda i, bexp, nused, tok: (i, 0)),
            scratch_shapes=[pltpu.VMEM((2, blk, d), F32), pltpu.SemaphoreType.DMA((2,))],
        ),
        out_shape=jax.ShapeDtypeStruct((n_slots, d), F32),
        compiler_params=_cparams("arbitrary"),
        name="moe_experts",
    )(disp["block_exp"], disp["n_used"], disp["slot_tok"], h2, disp["slot_gate"].reshape(n_slots, 1), w1, w3, w2)


def _moe_combine_kernel(pslot_ref, ys_hbm, x_ref, g2_ref, o_ref, buf_a, buf_b, sem):
    i = pl.program_id(0)
    tm = o_ref.shape[0]
    slot = lax.rem(i, 2)

    def start(tile, s):
        _start_row_gather(ys_hbm, pslot_ref, tile * 2 * tm, 2, buf_a, s, sem, tm)
        _start_row_gather(ys_hbm, pslot_ref, tile * 2 * tm + 1, 2, buf_b, s, sem, tm)

    @pl.when(i == 0)
    def _():
        start(0, 0)

    @pl.when(i + 1 < pl.num_programs(0))
    def _():
        start(i + 1, 1 - slot)

    _wait_row_gather(ys_hbm, buf_a, slot, sem, tm)
    _wait_row_gather(ys_hbm, buf_b, slot, sem, tm)
    o_ref[...] = x_ref[...] + g2_ref[0] * (buf_a[slot] + buf_b[slot])


def _moe_combine(ys, pair_slot, xa, modl, *, tm, nc_rows, seq):
    n, d = xa.shape
    seg = functools.partial(_seg_of_tile, tm=tm, nc_rows=nc_rows, seq=seq)
    return pl.pallas_call(
        _moe_combine_kernel,
        grid_spec=pltpu.PrefetchScalarGridSpec(
            num_scalar_prefetch=1,
            grid=(n // tm,),
            in_specs=[
                pl.BlockSpec(memory_space=pl.ANY),
                pl.BlockSpec((tm, d), lambda i, ps: (i, 0)),
                pl.BlockSpec((1, 1, d), lambda i, ps: (seg(i), 0, 5)),
            ],
            out_specs=pl.BlockSpec((tm, d), lambda i, ps: (i, 0)),
            scratch_shapes=[pltpu.VMEM((2, tm, d), F32), pltpu.VMEM((2, tm, d), F32), pltpu.SemaphoreType.DMA((2,))],
        ),
        out_shape=jax.ShapeDtypeStruct((n, d), F32),
        compiler_params=_cparams("arbitrary"),
        name="moe_combine",
    )(pair_slot, ys, xa, modl)


def _route(logits):
    lg = logits[:, :N_GROUPS]
    le = logits[:, N_GROUPS:N_GROUPS + N_EXPERTS].reshape(-1, N_GROUPS, EXP_PER_GROUP)
    p_g = jax.nn.softmax(lg, axis=-1)
    g_sel = jnp.argmax(lg, axis=-1).astype(jnp.int32)
    le = jnp.take_along_axis(le, g_sel[:, None, None], axis=1)[:, 0]
    top_p, top_i = lax.top_k(jax.nn.softmax(le, axis=-1), 2)
    gate = jnp.max(p_g, axis=-1, keepdims=True) * top_p / jnp.sum(top_p, axis=-1, keepdims=True)
    expert = (g_sel[:, None] * EXP_PER_GROUP + top_i).astype(jnp.int32)
    return expert, gate


def _dispatch(expert, gate, blk):
    m = expert.size
    e_flat = expert.reshape(m)
    onehot = (e_flat[:, None] == jnp.arange(N_EXPERTS, dtype=jnp.int32)[None, :]).astype(jnp.int32)
    csum = jnp.cumsum(onehot, axis=0)
    counts = csum[-1]
    rank = jnp.sum(csum * onehot, axis=1) - 1
    padded = (counts + blk - 1) // blk * blk
    pad_end = jnp.cumsum(padded)
    dest = ((pad_end - padded)[e_flat] + rank).astype(jnp.int32)
    n_blocks = -(-m // blk) + N_EXPERTS
    n_slots = n_blocks * blk
    token = jnp.arange(m, dtype=jnp.int32) // 2
    slot_tok = jnp.zeros((n_slots,), jnp.int32).at[dest].set(token, unique_indices=True)
    slot_gate = jnp.zeros((n_slots,), F32).at[dest].set(gate.reshape(m), unique_indices=True)
    block_exp = jnp.minimum(jnp.searchsorted(pad_end, jnp.arange(n_blocks, dtype=jnp.int32) * blk, side="right"),
                            N_EXPERTS - 1).astype(jnp.int32)
    n_used = (pad_end[-1] // blk).astype(jnp.int32).reshape(1)
    return dict(slot_tok=slot_tok, slot_gate=slot_gate, block_exp=block_exp, n_used=n_used, pair_slot=dest)


def _final_norm_kernel(x_ref, g_ref, o_ref):
    o_ref[...] = _rms(x_ref[...], g_ref[...])


def _final_norm(xa, g, *, tm, row_off, n_out):
    d = xa.shape[1]
    off = row_off // tm
    return pl.pallas_call(
        _final_norm_kernel,
        grid=(n_out // tm,),
        in_specs=[pl.BlockSpec((tm, d), lambda i: (i + off, 0)), pl.BlockSpec((1, d), lambda i: (0, 0))],
        out_specs=pl.BlockSpec((tm, d), lambda i: (i, 0)),
        out_shape=jax.ShapeDtypeStruct((n_out, d), F32),
        compiler_params=_cparams("arbitrary"),
        name="final_norm",
    )(xa, g.reshape(1, d))


def _block_diag2(m):
    z = jnp.zeros_like(m[0])
    return jnp.concatenate([jnp.concatenate([m[0], z], axis=1), jnp.concatenate([z, m[1]], axis=1)], axis=0)


def kernel(x, c, ctx, c_ctx, w_mod, b_mod, norm1_g, norm2_g, w_in, shift_mu, conv_w, w_up, w0, a_up, a0, g_up, k_k, k_a, r_k, lnx_g, lnx_b, w_a_out, w_b_out, w_o, router_g, router_g_b, router_e, router_e_b, exp_w1, exp_w3, exp_w2, final_g):
    bsz, seq, d = x.shape
    ctx_len = ctx.shape[1]
    depth = w_mod.shape[0]
    da = w_a_out.shape[1]
    db = w_b_out.shape[1]
    heads = db // HEAD
    assert bsz * heads == LANES and da == db and seq % ctx_len == 0 and ctx_len % GRID_W == 0
    assert bsz + 1 <= MOD_ROWS
    nc_rows = bsz * ctx_len
    n_all = nc_rows + bsz * seq
    t_all = ctx_len + seq
    tm = ctx_len
    tm_mm = min(512, seq)
    tc = min(SCAN_STEPS, ctx_len)
    tiles = dict(tm=tm, n_ctx_tiles=bsz, tiles_per_seq=seq // tm)
    segs = dict(nc_rows=nc_rows, seq=seq)

    s_all = jnp.zeros((MOD_ROWS, d), F32).at[0].set(c_ctx).at[1:1 + bsz].set(c)
    mod = _modulation(s_all, w_mod, b_mod)
    xa = jnp.concatenate([ctx.reshape(nc_rows, d), x.reshape(bsz * seq, d)], axis=0)
    ones_bd = jnp.kron(jnp.eye(heads, dtype=F32), jnp.ones((HEAD, HEAD), F32))
    rw0 = 3 * da
    rw1 = rw0 + 3 * db + 3 * LANES

    for l in range(depth):
        modl = mod[l].reshape(MOD_ROWS, 1, 6 * d)
        w_in_b = w_in[l].astype(BF16)
        router_w = jnp.zeros((d, ROUTER_COLS), F32)
        router_w = router_w.at[:, :N_GROUPS].set(router_g[l]).at[:, N_GROUPS:N_GROUPS + N_EXPERTS].set(router_e[l])
        router_b = jnp.zeros((1, ROUTER_COLS), F32)
        router_b = router_b.at[0, :N_GROUPS].set(router_g_b[l]).at[0, N_GROUPS:N_GROUPS + N_EXPERTS].set(router_e_b[l])
        lp = {
            "shift_mu": shift_mu[l], "k_k": k_k[l], "k_a": k_a[l], "r_k": r_k[l],
            "w0": w0[l], "a0": a0[l], "w_up_bd": _block_diag2(w_up[l]), "a_up_bd": _block_diag2(a_up[l]),
            "g_up": g_up[l], "norm2_g": norm2_g[l], "conv_w": conv_w[l], "lnx_g": lnx_g[l], "lnx_b": lnx_b[l],
            "w_a_out": w_a_out[l].astype(BF16), "w_b_out": w_b_out[l].astype(BF16), "w_o": w_o[l].astype(BF16),
            "router_w": router_w, "router_b": router_b,
        }
        zc, zr, zg = _in_proj(xa, norm1_g[l].reshape(1, d), modl, w_in_b[:, :rw0], w_in_b[:, rw0:rw1],
                              w_in_b[:, rw1:], tm=tm_mm, **segs)
        p, q = _prep(zr, lp, ones_bd, **tiles)
        p_c = p[:nc_rows].reshape(bsz, ctx_len, 9, heads, HEAD)
        p_l = p[nc_rows:].reshape(bsz, seq, 9, heads, HEAD)
        p_t = jnp.concatenate([p_c, p_l], axis=1).transpose(2, 1, 4, 0, 3).reshape(9, t_all, HEAD, LANES)
        y_t = _wkv_scan(p_t, ctx_len=ctx_len, tc=tc)
        y_bt = (y_t[0] + y_t[1]).reshape(t_all, HEAD, bsz, heads).transpose(2, 0, 3, 1)
        y = jnp.concatenate([y_bt[:, :ctx_len].reshape(nc_rows, db), y_bt[:, ctx_len:].reshape(bsz * seq, db)], axis=0)
        xa, h2, logits = _mix_out(xa, zc, y, q, zg, modl, lp, ones_bd, **tiles, **segs)
        expert, gate = _route(logits)
        disp = _dispatch(expert, gate, MOE_BLOCK)
        ys = _moe_experts(h2, disp, exp_w1[l].astype(BF16), exp_w3[l].astype(BF16), exp_w2[l].astype(BF16))
        xa = _moe_combine(ys, disp["pair_slot"], xa, modl, tm=tm, **segs)

    out = _final_norm(xa, final_g, tm=tm, row_off=nc_rows, n_out=bsz * seq)
    return out.reshape(bsz, seq, d)
```

```python
import functools

import jax
import jax.numpy as jnp
from jax import lax
from jax.experimental import pallas as pl
from jax.experimental.pallas import tpu as pltpu

F32 = jnp.float32
BF16 = jnp.bfloat16
HIGHEST = lax.Precision.HIGHEST

NORM_EPS = 1e-6
GN_EPS = 64e-5
KK_EPS = 1e-12

GRID_W = 64
HEAD = 64
N_GROUPS = 4
EXP_PER_GROUP = 8
N_EXPERTS = N_GROUPS * EXP_PER_GROUP
SUBLANES = 8
LANES = 128
MOD_ROWS = 24
ROUTER_COLS = LANES
SCAN_STEPS = 32
MOE_BLOCK = 256
N_SCAN_ARRAYS = 8
VMEM_LIMIT = 56 * 1024 * 1024


def _cparams(*sem):
    return pltpu.CompilerParams(dimension_semantics=sem, vmem_limit_bytes=VMEM_LIMIT)


def _sigmoid(x):
    return 1.0 / (1.0 + jnp.exp(-x))


def _dot_hi(a, b):
    return jnp.dot(a, b, precision=HIGHEST, preferred_element_type=F32)


def _dot_bf16(a, b):
    return jnp.dot(a.astype(BF16), b, preferred_element_type=F32)


def _rms(x, g):
    ms = jnp.mean(x * x, axis=-1, keepdims=True)
    return x * lax.rsqrt(ms + NORM_EPS) * g


def _mod_kernel(s_ref, w_ref, b_ref, o_ref):
    s = s_ref[...]
    s = s * _sigmoid(s)
    o_ref[0] = _dot_hi(s, w_ref[0]) + b_ref[0]


def _modulation(s_all, w_mod, b_mod):
    depth, d, n6 = w_mod.shape
    tn = d
    return pl.pallas_call(
        _mod_kernel,
        grid=(depth, n6 // tn),
        in_specs=[
            pl.BlockSpec((MOD_ROWS, d), lambda l, j: (0, 0)),
            pl.BlockSpec((1, d, tn), lambda l, j: (l, 0, j)),
            pl.BlockSpec((1, 1, tn), lambda l, j: (l, 0, j)),
        ],
        out_specs=pl.BlockSpec((1, MOD_ROWS, tn), lambda l, j: (l, 0, j)),
        out_shape=jax.ShapeDtypeStruct((depth, MOD_ROWS, n6), F32),
        compiler_params=_cparams("arbitrary", "arbitrary"),
        name="modulation",
    )(s_all, w_mod, b_mod.reshape(depth, 1, n6))


def _seg_of_tile(i, tm, nc_rows, seq):
    return jnp.where(i * tm < nc_rows, 0, 1 + (i * tm - nc_rows) // seq)


def _in_proj_kernel(x_ref, g_ref, sh_ref, sc_ref, wc_ref, wr_ref, wg_ref, zc_ref, zr_ref, zg_ref):
    h = _rms(x_ref[...], g_ref[...]) * (1.0 + sc_ref[0]) + sh_ref[0]
    hb = h.astype(BF16)
    zc_ref[...] = jnp.dot(hb, wc_ref[...], preferred_element_type=F32)
    zr_ref[...] = jnp.dot(hb, wr_ref[...], preferred_element_type=F32)
    zg_ref[...] = jnp.dot(hb, wg_ref[...], preferred_element_type=F32)


def _in_proj(xa, norm_g, modl, w_conv, w_rw, w_gate, *, tm, nc_rows, seq):
    n, d = xa.shape
    seg = functools.partial(_seg_of_tile, tm=tm, nc_rows=nc_rows, seq=seq)
    const = lambda shape: pl.BlockSpec(shape, lambda i: (0, 0), pipeline_mode=pl.Buffered(1))
    nc_, nr_, ng_ = w_conv.shape[1], w_rw.shape[1], w_gate.shape[1]
    return pl.pallas_call(
        _in_proj_kernel,
        grid=(n // tm,),
        in_specs=[
            pl.BlockSpec((tm, d), lambda i: (i, 0)),
            pl.BlockSpec((1, d), lambda i: (0, 0)),
            pl.BlockSpec((1, 1, d), lambda i: (seg(i), 0, 0)),
            pl.BlockSpec((1, 1, d), lambda i: (seg(i), 0, 1)),
            const((d, nc_)), const((d, nr_)), const((d, ng_)),
        ],
        out_specs=[
            pl.BlockSpec((tm, nc_), lambda i: (i, 0)),
            pl.BlockSpec((tm, nr_), lambda i: (i, 0)),
            pl.BlockSpec((tm, ng_), lambda i: (i, 0)),
        ],
        out_shape=[
            jax.ShapeDtypeStruct((n, nc_), F32),
            jax.ShapeDtypeStruct((n, nr_), F32),
            jax.ShapeDtypeStruct((n, ng_), F32),
        ],
        compiler_params=_cparams("arbitrary"),
        name="in_proj",
    )(xa, norm_g, modl, modl, w_conv, w_rw, w_gate)


def _prep_kernel(z_ref, zp_ref, zn_ref, mu_ref, kk_w_ref, ka_ref, rk_ref, w0_ref, a0_ref,
                 wup_ref, aup_ref, gup_ref, ones_ref, p_ref, q_ref, *, tm, n_ctx_tiles, tiles_per_seq):
    i = pl.program_id(0)
    is_ctx = i < n_ctx_tiles
    j = (i - n_ctx_tiles) % tiles_per_seq
    first = jnp.logical_or(is_ctx, j == 0)
    last = jnp.logical_or(is_ctx, j == tiles_per_seq - 1)
    z = z_ref[...]
    row = lax.broadcasted_iota(jnp.int32, z.shape, 0)
    prev_row = jnp.where(first, 0.0, zp_ref[SUBLANES - 1:SUBLANES, :])
    next_row = jnp.where(last, 0.0, zn_ref[0:1, :])
    z_prev = jnp.where(row == 0, prev_row, pltpu.roll(z, 1, axis=0))
    z_next = jnp.where(row == tm - 1, next_row, pltpu.roll(z, tm - 1, axis=0))
    zs = z + mu_ref[...] * (0.5 * (z_prev + z_next) - z)

    db = ones_ref.shape[0]
    r = zs[:, 0:db]
    k = zs[:, db:2 * db]
    v = zs[:, 2 * db:3 * db]
    o = 3 * db
    zw = zs[:, o:o + LANES]
    za = zs[:, o + LANES:o + 2 * LANES]
    zg = zs[:, o + 2 * LANES:o + 3 * LANES]

    kk = k * kk_w_ref[...]
    kk = kk * lax.rsqrt(_dot_hi(kk * kk, ones_ref[...]) + KK_EPS)
    u = w0_ref[...] + _dot_hi(jnp.tanh(zw), wup_ref[...])
    nu = -u
    softplus = jnp.maximum(nu, 0.0) + jnp.log(1.0 + jnp.exp(-jnp.abs(nu)))
    decay = jnp.exp(-jnp.exp(-softplus - 0.5))
    a = _sigmoid(a0_ref[...] + _dot_hi(za, aup_ref[...]))
    ka = ka_ref[...]
    a_f, a_b = a[:, :db], a[:, db:]
    kd_f = k * (1.0 + (a_f - 1.0) * ka)
    kd_b = k * (1.0 + (a_b - 1.0) * ka)
    bonus = _dot_hi(r * (kd_f + kd_b) * rk_ref[...], ones_ref[...]) * v
    gate = _dot_hi(_sigmoid(zg), gup_ref[...])

    p_ref[:, 0 * db:1 * db] = kk
    p_ref[:, 1 * db:2 * db] = r
    p_ref[:, 2 * db:3 * db] = decay[:, :db]
    p_ref[:, 3 * db:4 * db] = kd_f
    p_ref[:, 4 * db:5 * db] = kk * a_f
    p_ref[:, 5 * db:6 * db] = decay[:, db:]
    p_ref[:, 6 * db:7 * db] = kd_b
    p_ref[:, 7 * db:8 * db] = kk * a_b
    q_ref[:, 0:db] = v
    q_ref[:, db:2 * db] = bonus
    q_ref[:, 2 * db:3 * db] = gate


def _prep(zr, lp, ones_bd, *, tm, n_ctx_tiles, tiles_per_seq):
    n, nr = zr.shape
    db = ones_bd.shape[0]
    hb = tm // SUBLANES
    nblk8 = n // SUBLANES
    row = lambda a: a.reshape(1, -1)
    const = lambda a: pl.BlockSpec(a.shape, lambda i: (0,) * a.ndim)
    params = [row(lp["shift_mu"]), row(lp["k_k"]), row(lp["k_a"]), row(lp["r_k"]),
              row(lp["w0"]), row(lp["a0"]), lp["w_up_bd"], lp["a_up_bd"], lp["g_up"], ones_bd]
    kern = functools.partial(_prep_kernel, tm=tm, n_ctx_tiles=n_ctx_tiles, tiles_per_seq=tiles_per_seq)
    return pl.pallas_call(
        kern,
        grid=(n // tm,),
        in_specs=[
            pl.BlockSpec((tm, nr), lambda i: (i, 0)),
            pl.BlockSpec((SUBLANES, nr), lambda i: (jnp.maximum(i * hb - 1, 0), 0)),
            pl.BlockSpec((SUBLANES, nr), lambda i: (jnp.minimum((i + 1) * hb, nblk8 - 1), 0)),
        ] + [const(a) for a in params],
        out_specs=[
            pl.BlockSpec((tm, N_SCAN_ARRAYS * db), lambda i: (i, 0)),
            pl.BlockSpec((tm, 3 * db), lambda i: (i, 0)),
        ],
        out_shape=[
            jax.ShapeDtypeStruct((n, N_SCAN_ARRAYS * db), F32),
            jax.ShapeDtypeStruct((n, 3 * db), F32),
        ],
        compiler_params=_cparams("arbitrary"),
        name="rwkv_prep",
    )(zr, zr, zr, *params)


def _relayout_kernel(*refs):
    x_refs, o_ref, a_scr = refs[:-2], refs[-2], refs[-1]
    nk = o_ref.shape[0]
    tiles = x_refs[0].shape[1] // LANES
    for b, x_ref in enumerate(x_refs):
        for hp in range(tiles):
            a_scr[pl.ds((b * tiles + hp) * LANES, LANES), :] = x_ref[:, hp * LANES:(hp + 1) * LANES].T
    for k in range(nk):
        o_ref[k] = a_scr[pl.ds(k, LANES, stride=nk), :].T


def _relayout(p, *, bsz, ctx_len, seq):
    n, w_all = p.shape
    width = w_all // N_SCAN_ARRAYS
    heads = width // HEAD
    assert bsz * heads == LANES and ctx_len % LANES == 0 and seq % LANES == 0
    t_all = ctx_len + seq
    ncb = ctx_len // LANES
    nc_blocks = bsz * ncb

    def row_block(b):
        return lambda a, tb: (jnp.where(tb < ncb, b * ncb + tb, nc_blocks + b * (seq // LANES) + tb - ncb), a)

    return pl.pallas_call(
        _relayout_kernel,
        grid=(N_SCAN_ARRAYS, t_all // LANES),
        in_specs=[pl.BlockSpec((LANES, width), row_block(b)) for b in range(bsz)],
        out_specs=pl.BlockSpec((None, HEAD, LANES, LANES), lambda a, tb: (a, 0, tb, 0)),
        out_shape=jax.ShapeDtypeStruct((N_SCAN_ARRAYS, HEAD, t_all, LANES), F32),
        scratch_shapes=[pltpu.VMEM((LANES * HEAD, LANES), F32)],
        compiler_params=_cparams("arbitrary", "arbitrary"),
        name="scan_relayout",
    )(*([p] * bsz))


def _row_bcast(ref, t, k):
    return jnp.broadcast_to(ref[k, pl.ds(t, 1), :], (SUBLANES, LANES))


def _wkv_kernel(kk_ref, r_ref, v_ref, w_ref, kd_ref, b_ref, y_ref, s_scr, sa_scr, *, tc):
    d = pl.program_id(0)
    nk = s_scr.shape[0]
    nvb = s_scr.shape[1] // SUBLANES
    vrows = lambda vb: pl.ds(vb * SUBLANES, SUBLANES)

    @pl.when(pl.program_id(1) == 0)
    def _():
        s_scr[...] = jnp.zeros_like(s_scr)

    t_first = d * (tc - 1)
    acc = [jnp.zeros((SUBLANES, LANES), F32) for _ in range(nvb)]
    for k in range(nk):
        kk0 = _row_bcast(kk_ref, t_first, k)
        for vb in range(nvb):
            acc[vb] = acc[vb] + s_scr[k, vrows(vb), :] * kk0
    for vb in range(nvb):
        sa_scr[vrows(vb), :] = acc[vb]

    def step(i, carry):
        t = i + d * (tc - 1 - 2 * i)
        tn = jnp.clip(t + 1 - 2 * d, 0, tc - 1)
        sa = [sa_scr[vrows(vb), :] for vb in range(nvb)]
        vv = [v_ref[t, vrows(vb), :] for vb in range(nvb)]
        y = [jnp.zeros((SUBLANES, LANES), F32) for _ in range(nvb)]
        san = [jnp.zeros((SUBLANES, LANES), F32) for _ in range(nvb)]
        for k in range(nk):
            wk = _row_bcast(w_ref, t, k)
            bk = _row_bcast(b_ref, t, k)
            kdk = _row_bcast(kd_ref, t, k)
            rk = _row_bcast(r_ref, t, k)
            kkn = _row_bcast(kk_ref, tn, k)
            for vb in range(nvb):
                s = s_scr[k, vrows(vb), :]
                s = s * wk - sa[vb] * bk + vv[vb] * kdk
                s_scr[k, vrows(vb), :] = s
                y[vb] = y[vb] + s * rk
                san[vb] = san[vb] + s * kkn
        for vb in range(nvb):
            y_ref[t, vrows(vb), :] = y[vb]
            sa_scr[vrows(vb), :] = san[vb]
        return carry

    lax.fori_loop(0, tc, step, 0)


def _wkv_scan(p_t, v_t, *, ctx_len, tc):
    _, nk, t_all, lanes = p_t.shape
    nb = t_all // tc
    nbc = ctx_len // tc

    def blk(d, j):
        rev = jnp.where(j < nbc, nbc - 1 - j, nb - 1 - j + nbc)
        return jnp.where(d == 0, j, rev)

    shared = lambda idx: pl.BlockSpec((None, nk, tc, lanes), lambda d, j: (idx, 0, blk(d, j), 0))
    per_dir = lambda idx: pl.BlockSpec((None, nk, tc, lanes), lambda d, j: (idx + 3 * d, 0, blk(d, j), 0))
    return pl.pallas_call(
        functools.partial(_wkv_kernel, tc=tc),
        grid=(2, nb),
        in_specs=[shared(0), shared(1), pl.BlockSpec((tc, nk, lanes), lambda d, j: (blk(d, j), 0, 0)),
                  per_dir(2), per_dir(3), per_dir(4)],
        out_specs=pl.BlockSpec((None, tc, nk, lanes), lambda d, j: (d, blk(d, j), 0, 0)),
        out_shape=jax.ShapeDtypeStruct((2, t_all, nk, lanes), F32),
        scratch_shapes=[pltpu.VMEM((nk, nk, lanes), F32), pltpu.VMEM((nk, lanes), F32)],
        compiler_params=_cparams("arbitrary", "arbitrary"),
        name="wkv_scan",
    )(p_t, p_t, v_t, p_t, p_t, p_t)


def _mix_out_kernel(x_ref, zc_ref, zcp_ref, zcn_ref, y_ref, q_ref, zg_ref,
                    g1_ref, sh2_ref, sc2_ref, n2_ref, cw_ref, lng_ref, lnb_ref, ones_ref,
                    wa_ref, wb_ref, wo_ref, wr_ref, br_ref,
                    xo_ref, h2_ref, lg_ref, *, tm, n_ctx_tiles, tiles_per_seq):
    i = pl.program_id(0)
    is_ctx = i < n_ctx_tiles
    j = (i - n_ctx_tiles) % tiles_per_seq
    first = jnp.logical_or(is_ctx, j == 0)
    last = jnp.logical_or(is_ctx, j == tiles_per_seq - 1)
    da = ones_ref.shape[0]
    half = da // 2

    zc = zc_ref[...]
    bg, u = zc[:, 0:da], zc[:, da:2 * da] * zc[:, 2 * da:3 * da]
    row = lax.broadcasted_iota(jnp.int32, u.shape, 0)
    period = jnp.where(is_ctx, tm, GRID_W)
    pos = jnp.bitwise_and(row, period - 1)
    u_m1 = jnp.where(pos == 0, 0.0, pltpu.roll(u, 1, axis=0))
    u_p1 = jnp.where(pos == period - 1, 0.0, pltpu.roll(u, tm - 1, axis=0))
    uh = u[:, half:]
    hp = zcp_ref[:, da + half:2 * da] * zcp_ref[:, 2 * da + half:3 * da]
    hn = zcn_ref[:, da + half:2 * da] * zcn_ref[:, 2 * da + half:3 * da]
    hp = jnp.where(first, 0.0, hp)
    hn = jnp.where(last, 0.0, hn)
    uv_m = jnp.concatenate([hp, uh[:tm - GRID_W]], axis=0)
    uv_p = jnp.concatenate([uh[GRID_W:], hn], axis=0)
    um_hi = jnp.where(is_ctx, u_m1[:, half:], uv_m)
    up_hi = jnp.where(is_ctx, u_p1[:, half:], uv_p)
    cw = cw_ref[...]
    conv_lo = u_m1[:, :half] * cw[0:1, :half] + u[:, :half] * cw[1:2, :half] + u_p1[:, :half] * cw[2:3, :half]
    conv_hi = um_hi * cw[0:1, half:] + uh * cw[1:2, half:] + up_hi * cw[2:3, half:]
    ya = (_dot_bf16(bg[:, :half] * conv_lo, wa_ref[0:half, :])
          + _dot_bf16(bg[:, half:] * conv_hi, wa_ref[half:, :]))

    y = y_ref[...]
    inv_n = 1.0 / HEAD
    mean = _dot_hi(y, ones_ref[...]) * inv_n
    yc = y - mean
    var = _dot_hi(yc * yc, ones_ref[...]) * inv_n
    yn = yc * lax.rsqrt(var + GN_EPS) * lng_ref[...] + lnb_ref[...]
    db = y.shape[1]
    yb = _dot_bf16((yn + q_ref[:, db:2 * db]) * q_ref[:, 2 * db:3 * db], wb_ref[...])

    d = x_ref.shape[1]
    merged = _sigmoid(zg_ref[:, 0:d]) * ya + _sigmoid(zg_ref[:, d:2 * d]) * yb
    x_new = x_ref[...] + g1_ref[0] * _dot_bf16(merged, wo_ref[...])
    xo_ref[...] = x_new

    h2 = _rms(x_new, n2_ref[...]) * (1.0 + sc2_ref[0]) + sh2_ref[0]
    h2_ref[...] = h2
    lg_ref[...] = _dot_hi(h2, wr_ref[...]) + br_ref[...]


def _mix_out(xa, zc, y, q, zg, modl, lp, ones_bd, *, tm, n_ctx_tiles, tiles_per_seq, nc_rows, seq):
    n, d = xa.shape
    da3 = zc.shape[1]
    db = y.shape[1]
    hb = tm // GRID_W
    nblk = n // GRID_W
    seg = functools.partial(_seg_of_tile, tm=tm, nc_rows=nc_rows, seq=seq)
    mod = lambda c: pl.BlockSpec((1, 1, d), lambda i: (seg(i), 0, c))
    const = lambda a: pl.BlockSpec(a.shape, lambda i: (0,) * a.ndim)
    tile = lambda w: pl.BlockSpec((tm, w), lambda i: (i, 0))
    params = [lp["norm2_g"].reshape(1, d), lp["conv_w"], lp["lnx_g"].reshape(1, db), lp["lnx_b"].reshape(1, db),
              ones_bd, lp["w_a_out"], lp["w_b_out"], lp["w_o"], lp["router_w"], lp["router_b"]]
    kern = functools.partial(_mix_out_kernel, tm=tm, n_ctx_tiles=n_ctx_tiles, tiles_per_seq=tiles_per_seq)
    return pl.pallas_call(
        kern,
        grid=(n // tm,),
        in_specs=[
            tile(d), tile(da3),
            pl.BlockSpec((GRID_W, da3), lambda i: (jnp.maximum(i * hb - 1, 0), 0)),
            pl.BlockSpec((GRID_W, da3), lambda i: (jnp.minimum((i + 1) * hb, nblk - 1), 0)),
            tile(db), tile(3 * db), tile(2 * d),
            mod(2), mod(3), mod(4),
        ] + [const(a) for a in params],
        out_specs=[tile(d), tile(d), tile(ROUTER_COLS)],
        out_shape=[
            jax.ShapeDtypeStruct((n, d), F32),
            jax.ShapeDtypeStruct((n, d), F32),
            jax.ShapeDtypeStruct((n, ROUTER_COLS), F32),
        ],
        compiler_params=_cparams("arbitrary"),
        name="mix_out",
    )(xa, zc, zc, zc, y, q, zg, modl, modl, modl, *params)


def _moe_dispatch_kernel(pslot_ref, h_ref, init_hbm, o_hbm, sem):
    del init_hbm
    i = pl.program_id(0)
    tm = h_ref.shape[0]

    def row_copy(r, s):
        return pltpu.make_async_copy(h_ref.at[pl.ds(r, 1)], o_hbm.at[pl.ds(s, 1)], sem.at[0])

    def start(r, carry):
        p = (i * tm + r) * 2
        row_copy(r, pslot_ref[p]).start()
        row_copy(r, pslot_ref[p + 1]).start()
        return carry

    def wait(r, carry):
        row_copy(r, 0).wait()
        row_copy(r, 0).wait()
        return carry

    lax.fori_loop(0, tm, start, 0, unroll=4)
    lax.fori_loop(0, tm, wait, 0, unroll=4)


def _moe_dispatch(h2, pair_slot, n_slots, *, tm):
    n, d = h2.shape
    return pl.pallas_call(
        _moe_dispatch_kernel,
        grid_spec=pltpu.PrefetchScalarGridSpec(
            num_scalar_prefetch=1,
            grid=(n // tm,),
            in_specs=[pl.BlockSpec((tm, d), lambda i, ps: (i, 0)), pl.BlockSpec(memory_space=pl.ANY)],
            out_specs=pl.BlockSpec(memory_space=pl.ANY),
            scratch_shapes=[pltpu.SemaphoreType.DMA((1,))],
        ),
        out_shape=jax.ShapeDtypeStruct((n_slots, d), F32),
        input_output_aliases={2: 0},
        compiler_params=_cparams("arbitrary"),
        name="moe_dispatch",
    )(pair_slot, h2, jnp.zeros((n_slots, d), F32))


def _start_row_gather(src_hbm, idx_ref, base, stride, buf, slot, sem, n_rows):
    def body(r, carry):
        row = idx_ref[base + r * stride]
        pltpu.make_async_copy(src_hbm.at[pl.ds(row, 1)], buf.at[slot, pl.ds(r, 1)], sem.at[slot]).start()
        return carry
    lax.fori_loop(0, n_rows, body, 0, unroll=8)


def _wait_row_gather(src_hbm, buf, slot, sem, n_rows):
    def body(r, carry):
        pltpu.make_async_copy(src_hbm.at[pl.ds(0, 1)], buf.at[slot, pl.ds(r, 1)], sem.at[slot]).wait()
        return carry
    lax.fori_loop(0, n_rows, body, 0, unroll=8)


def _moe_expert_kernel(bexp_ref, nused_ref, h_ref, w1_ref, w3_ref, w2_ref, o_ref, w1b, w3b, w2b):
    i = pl.program_id(0)
    new_expert = jnp.logical_or(i == 0, bexp_ref[i] != bexp_ref[jnp.maximum(i - 1, 0)])

    @pl.when(new_expert)
    def _():
        w1b[...] = w1_ref[0].astype(BF16)
        w3b[...] = w3_ref[0].astype(BF16)
        w2b[...] = w2_ref[0].astype(BF16)

    @pl.when(i < nused_ref[0])
    def _():
        h = h_ref[...].astype(BF16)
        a = jnp.dot(h, w1b[...], preferred_element_type=F32)
        b = jnp.dot(h, w3b[...], preferred_element_type=F32)
        hid = a * _sigmoid(a) * b
        o_ref[...] = _dot_bf16(hid, w2b[...])

    @pl.when(i >= nused_ref[0])
    def _():
        o_ref[...] = jnp.zeros_like(o_ref)


def _moe_experts(hs, block_exp, n_used, w1, w3, w2):
    n_slots, d = hs.shape
    _, _, f = w1.shape
    blk = MOE_BLOCK
    wspec = lambda shape: pl.BlockSpec((1,) + shape, lambda i, bexp, nused: (bexp[i], 0, 0))
    return pl.pallas_call(
        _moe_expert_kernel,
        grid_spec=pltpu.PrefetchScalarGridSpec(
            num_scalar_prefetch=2,
            grid=(n_slots // blk,),
            in_specs=[
                pl.BlockSpec((blk, d), lambda i, bexp, nused: (i, 0)),
                wspec((d, f)), wspec((d, f)), wspec((f, d)),
            ],
            out_specs=pl.BlockSpec((blk, d), lambda i, bexp, nused: (i, 0)),
            scratch_shapes=[pltpu.VMEM((d, f), BF16), pltpu.VMEM((d, f), BF16), pltpu.VMEM((f, d), BF16)],
        ),
        out_shape=jax.ShapeDtypeStruct((n_slots, d), F32),
        compiler_params=_cparams("arbitrary"),
        name="moe_experts",
    )(block_exp, n_used, hs, w1, w3, w2)


def _moe_combine_kernel(pslot_ref, ys_hbm, x_ref, gate_ref, g2_ref, o_ref, buf_a, buf_b, sem):
    i = pl.program_id(0)
    tm = o_ref.shape[0]
    slot = lax.rem(i, 2)

    def start(tile, s):
        _start_row_gather(ys_hbm, pslot_ref, tile * 2 * tm, 2, buf_a, s, sem, tm)
        _start_row_gather(ys_hbm, pslot_ref, tile * 2 * tm + 1, 2, buf_b, s, sem, tm)

    @pl.when(i == 0)
    def _():
        start(0, 0)

    @pl.when(i + 1 < pl.num_programs(0))
    def _():
        start(i + 1, 1 - slot)

    _wait_row_gather(ys_hbm, buf_a, slot, sem, tm)
    _wait_row_gather(ys_hbm, buf_b, slot, sem, tm)
    gate = gate_ref[...]
    y2 = gate[:, 0:1] * buf_a[slot] + gate[:, 1:2] * buf_b[slot]
    o_ref[...] = x_ref[...] + g2_ref[0] * y2


def _moe_combine(ys, pair_slot, gate, xa, modl, *, tm, nc_rows, seq):
    n, d = xa.shape
    seg = functools.partial(_seg_of_tile, tm=tm, nc_rows=nc_rows, seq=seq)
    return pl.pallas_call(
        _moe_combine_kernel,
        grid_spec=pltpu.PrefetchScalarGridSpec(
            num_scalar_prefetch=1,
            grid=(n // tm,),
            in_specs=[
                pl.BlockSpec(memory_space=pl.ANY),
                pl.BlockSpec((tm, d), lambda i, ps: (i, 0)),
                pl.BlockSpec((tm, 2), lambda i, ps: (i, 0)),
                pl.BlockSpec((1, 1, d), lambda i, ps: (seg(i), 0, 5)),
            ],
            out_specs=pl.BlockSpec((tm, d), lambda i, ps: (i, 0)),
            scratch_shapes=[pltpu.VMEM((2, tm, d), F32), pltpu.VMEM((2, tm, d), F32), pltpu.SemaphoreType.DMA((2,))],
        ),
        out_shape=jax.ShapeDtypeStruct((n, d), F32),
        compiler_params=_cparams("arbitrary"),
        name="moe_combine",
    )(pair_slot, ys, xa, gate, modl)


def _route(logits):
    lg = logits[:, :N_GROUPS]
    le = logits[:, N_GROUPS:N_GROUPS + N_EXPERTS].reshape(-1, N_GROUPS, EXP_PER_GROUP)
    p_g = jax.nn.softmax(lg, axis=-1)
    g_sel = jnp.argmax(lg, axis=-1).astype(jnp.int32)
    le = jnp.take_along_axis(le, g_sel[:, None, None], axis=1)[:, 0]
    top_p, top_i = lax.top_k(jax.nn.softmax(le, axis=-1), 2)
    gate = jnp.max(p_g, axis=-1, keepdims=True) * top_p / jnp.sum(top_p, axis=-1, keepdims=True)
    expert = (g_sel[:, None] * EXP_PER_GROUP + top_i).astype(jnp.int32)
    return expert, gate


def _dispatch(expert, blk):
    m = expert.size
    e_flat = expert.reshape(m)
    onehot = (e_flat[:, None] == jnp.arange(N_EXPERTS, dtype=jnp.int32)[None, :]).astype(jnp.int32)
    csum = jnp.cumsum(onehot, axis=0)
    counts = csum[-1]
    rank = jnp.sum(csum * onehot, axis=1) - 1
    padded = (counts + blk - 1) // blk * blk
    pad_end = jnp.cumsum(padded)
    dest = ((pad_end - padded)[e_flat] + rank).astype(jnp.int32)
    n_blocks = -(-m // blk) + N_EXPERTS
    block_exp = jnp.minimum(jnp.searchsorted(pad_end, jnp.arange(n_blocks, dtype=jnp.int32) * blk, side="right"),
                            N_EXPERTS - 1).astype(jnp.int32)
    n_used = (pad_end[-1] // blk).astype(jnp.int32).reshape(1)
    return dest, block_exp, n_used, n_blocks * blk


def _final_norm_kernel(x_ref, g_ref, o_ref):
    o_ref[...] = _rms(x_ref[...], g_ref[...])


def _final_norm(xa, g, *, tm, row_off, n_out):
    d = xa.shape[1]
    off = row_off // tm
    return pl.pallas_call(
        _final_norm_kernel,
        grid=(n_out // tm,),
        in_specs=[pl.BlockSpec((tm, d), lambda i: (i + off, 0)), pl.BlockSpec((1, d), lambda i: (0, 0))],
        out_specs=pl.BlockSpec((tm, d), lambda i: (i, 0)),
        out_shape=jax.ShapeDtypeStruct((n_out, d), F32),
        compiler_params=_cparams("arbitrary"),
        name="final_norm",
    )(xa, g.reshape(1, d))


def _block_diag2(m):
    z = jnp.zeros_like(m[0])
    return jnp.concatenate([jnp.concatenate([m[0], z], axis=1), jnp.concatenate([z, m[1]], axis=1)], axis=0)


def kernel(x, c, ctx, c_ctx, w_mod, b_mod, norm1_g, norm2_g, w_in, shift_mu, conv_w, w_up, w0, a_up, a0, g_up, k_k, k_a, r_k, lnx_g, lnx_b, w_a_out, w_b_out, w_o, router_g, router_g_b, router_e, router_e_b, exp_w1, exp_w3, exp_w2, final_g):
    bsz, seq, d = x.shape
    ctx_len = ctx.shape[1]
    depth = w_mod.shape[0]
    da = w_a_out.shape[1]
    db = w_b_out.shape[1]
    heads = db // HEAD
    assert bsz * heads == LANES and da == db and seq % ctx_len == 0 and ctx_len % GRID_W == 0
    assert bsz + 1 <= MOD_ROWS
    nc_rows = bsz * ctx_len
    n_all = nc_rows + bsz * seq
    t_all = ctx_len + seq
    tm = ctx_len
    tm_mm = min(512, seq)
    tc = min(SCAN_STEPS, ctx_len)
    tiles = dict(tm=tm, n_ctx_tiles=bsz, tiles_per_seq=seq // tm)
    segs = dict(nc_rows=nc_rows, seq=seq)

    s_all = jnp.zeros((MOD_ROWS, d), F32).at[0].set(c_ctx).at[1:1 + bsz].set(c)
    mod = _modulation(s_all, w_mod, b_mod)
    xa = jnp.concatenate([ctx.reshape(nc_rows, d), x.reshape(bsz * seq, d)], axis=0)
    ones_bd = jnp.kron(jnp.eye(heads, dtype=F32), jnp.ones((HEAD, HEAD), F32))
    rw0 = 3 * da
    rw1 = rw0 + 3 * db + 3 * LANES

    for l in range(depth):
        modl = mod[l].reshape(MOD_ROWS, 1, 6 * d)
        w_in_b = w_in[l].astype(BF16)
        router_w = jnp.zeros((d, ROUTER_COLS), F32)
        router_w = router_w.at[:, :N_GROUPS].set(router_g[l]).at[:, N_GROUPS:N_GROUPS + N_EXPERTS].set(router_e[l])
        router_b = jnp.zeros((1, ROUTER_COLS), F32)
        router_b = router_b.at[0, :N_GROUPS].set(router_g_b[l]).at[0, N_GROUPS:N_GROUPS + N_EXPERTS].set(router_e_b[l])
        lp = {
            "shift_mu": shift_mu[l], "k_k": k_k[l], "k_a": k_a[l], "r_k": r_k[l],
            "w0": w0[l], "a0": a0[l], "w_up_bd": _block_diag2(w_up[l]), "a_up_bd": _block_diag2(a_up[l]),
            "g_up": g_up[l], "norm2_g": norm2_g[l], "conv_w": conv_w[l], "lnx_g": lnx_g[l], "lnx_b": lnx_b[l],
            "w_a_out": w_a_out[l].astype(BF16), "w_b_out": w_b_out[l].astype(BF16), "w_o": w_o[l].astype(BF16),
            "router_w": router_w, "router_b": router_b,
        }
        zc, zr, zg = _in_proj(xa, norm1_g[l].reshape(1, d), modl, w_in_b[:, :rw0], w_in_b[:, rw0:rw1],
                              w_in_b[:, rw1:], tm=tm_mm, **segs)
        p, q = _prep(zr, lp, ones_bd, **tiles)
        p_t = _relayout(p, bsz=bsz, ctx_len=ctx_len, seq=seq)
        v_c = q[:nc_rows, :db].reshape(bsz, ctx_len, heads, HEAD)
        v_l = q[nc_rows:, :db].reshape(bsz, seq, heads, HEAD)
        v_t = jnp.concatenate([v_c, v_l], axis=1).transpose(1, 3, 0, 2).reshape(t_all, HEAD, LANES)
        y_t = _wkv_scan(p_t, v_t, ctx_len=ctx_len, tc=tc)
        y_bt = (y_t[0] + y_t[1]).reshape(t_all, HEAD, bsz, heads).transpose(2, 0, 3, 1)
        y = jnp.concatenate([y_bt[:, :ctx_len].reshape(nc_rows, db), y_bt[:, ctx_len:].reshape(bsz * seq, db)], axis=0)
        xa, h2, logits = _mix_out(xa, zc, y, q, zg, modl, lp, ones_bd, **tiles, **segs)
        expert, gate = _route(logits)
        pair_slot, block_exp, n_used, n_slots = _dispatch(expert, MOE_BLOCK)
        hs = _moe_dispatch(h2, pair_slot, n_slots, tm=tm)
        ys = _moe_experts(hs, block_exp, n_used, exp_w1[l], exp_w3[l], exp_w2[l])
        xa = _moe_combine(ys, pair_slot, gate, xa, modl, tm=tm, **segs)

    out = _final_norm(xa, final_g, tm=tm, row_off=nc_rows, n_out=bsz * seq)
    return out.reshape(bsz, seq, d)
```

```python
import functools

import jax
import jax.numpy as jnp
from jax import lax
from jax.experimental import pallas as pl
from jax.experimental.pallas import tpu as pltpu

F32 = jnp.float32
BF16 = jnp.bfloat16
HIGHEST = lax.Precision.HIGHEST

NORM_EPS = 1e-6
GN_EPS = 64e-5
KK_EPS = 1e-12

GRID_W = 64
HEAD = 64
N_GROUPS = 4
EXP_PER_GROUP = 8
N_EXPERTS = N_GROUPS * EXP_PER_GROUP
SUBLANES = 8
LANES = 128
MOD_ROWS = 24
ROUTER_COLS = LANES
SCAN_STEPS = 64
MOE_BLOCK = 256
N_SCAN_ARRAYS = 9
VMEM_LIMIT = 56 * 1024 * 1024


def _cparams(*sem):
    return pltpu.CompilerParams(dimension_semantics=sem, vmem_limit_bytes=VMEM_LIMIT)


def _sigmoid(x):
    return 1.0 / (1.0 + jnp.exp(-x))


def _dot_hi(a, b):
    return jnp.dot(a, b, precision=HIGHEST, preferred_element_type=F32)


def _dot_bf16(a, b):
    return jnp.dot(a.astype(BF16), b, preferred_element_type=F32)


def _split_bf16(x):
    hi = x.astype(BF16)
    lo = (x - hi.astype(F32)).astype(BF16)
    return hi, lo


def _sum_heads(x, ones_bd):
    hi, lo = _split_bf16(x)
    return (jnp.dot(hi, ones_bd, preferred_element_type=F32)
            + jnp.dot(lo, ones_bd, preferred_element_type=F32))


def _rms(x, g):
    ms = jnp.mean(x * x, axis=-1, keepdims=True)
    return x * lax.rsqrt(ms + NORM_EPS) * g


def _mod_kernel(s_ref, w_ref, b_ref, o_ref):
    s = s_ref[...]
    s = s * _sigmoid(s)
    o_ref[0] = _dot_hi(s, w_ref[0]) + b_ref[0]


def _modulation(s_all, w_mod, b_mod):
    depth, d, n6 = w_mod.shape
    tn = d
    return pl.pallas_call(
        _mod_kernel,
        grid=(depth, n6 // tn),
        in_specs=[
            pl.BlockSpec((MOD_ROWS, d), lambda l, j: (0, 0)),
            pl.BlockSpec((1, d, tn), lambda l, j: (l, 0, j)),
            pl.BlockSpec((1, 1, tn), lambda l, j: (l, 0, j)),
        ],
        out_specs=pl.BlockSpec((1, MOD_ROWS, tn), lambda l, j: (l, 0, j)),
        out_shape=jax.ShapeDtypeStruct((depth, MOD_ROWS, n6), F32),
        compiler_params=_cparams("arbitrary", "arbitrary"),
        name="modulation",
    )(s_all, w_mod, b_mod.reshape(depth, 1, n6))


def _seg_of_tile(i, tm, nc_rows, seq):
    return jnp.where(i * tm < nc_rows, 0, 1 + (i * tm - nc_rows) // seq)


def _in_proj_kernel(x_ref, g_ref, sh_ref, sc_ref, wc_ref, wr_ref, wg_ref, zc_ref, zr_ref, zg_ref):
    h = _rms(x_ref[...], g_ref[...]) * (1.0 + sc_ref[0]) + sh_ref[0]
    hb = h.astype(BF16)
    zc_ref[...] = jnp.dot(hb, wc_ref[...], preferred_element_type=F32)
    zr_ref[...] = jnp.dot(hb, wr_ref[...], preferred_element_type=F32)
    zg_ref[...] = jnp.dot(hb, wg_ref[...], preferred_element_type=F32)


def _in_proj(xa, norm_g, modl, w_conv, w_rw, w_gate, *, tm, nc_rows, seq):
    n, d = xa.shape
    seg = functools.partial(_seg_of_tile, tm=tm, nc_rows=nc_rows, seq=seq)
    const = lambda shape: pl.BlockSpec(shape, lambda i: (0, 0), pipeline_mode=pl.Buffered(1))
    nc_, nr_, ng_ = w_conv.shape[1], w_rw.shape[1], w_gate.shape[1]
    return pl.pallas_call(
        _in_proj_kernel,
        grid=(n // tm,),
        in_specs=[
            pl.BlockSpec((tm, d), lambda i: (i, 0)),
            pl.BlockSpec((1, d), lambda i: (0, 0)),
            pl.BlockSpec((1, 1, d), lambda i: (seg(i), 0, 0)),
            pl.BlockSpec((1, 1, d), lambda i: (seg(i), 0, 1)),
            const((d, nc_)), const((d, nr_)), const((d, ng_)),
        ],
        out_specs=[
            pl.BlockSpec((tm, nc_), lambda i: (i, 0)),
            pl.BlockSpec((tm, nr_), lambda i: (i, 0)),
            pl.BlockSpec((tm, ng_), lambda i: (i, 0)),
        ],
        out_shape=[
            jax.ShapeDtypeStruct((n, nc_), F32),
            jax.ShapeDtypeStruct((n, nr_), F32),
            jax.ShapeDtypeStruct((n, ng_), F32),
        ],
        compiler_params=_cparams("arbitrary"),
        name="in_proj",
    )(xa, norm_g, modl, modl, w_conv, w_rw, w_gate)


def _prep_kernel(z_ref, zp_ref, zn_ref, mu_ref, kk_w_ref, ka_ref, rk_ref, w0_ref, a0_ref,
                 wup_ref, aup_ref, gup_ref, ones_ref, p_ref, q_ref, *, tm, n_ctx_tiles, tiles_per_seq):
    i = pl.program_id(0)
    is_ctx = i < n_ctx_tiles
    j = (i - n_ctx_tiles) % tiles_per_seq
    first = jnp.logical_or(is_ctx, j == 0)
    last = jnp.logical_or(is_ctx, j == tiles_per_seq - 1)
    z = z_ref[...]
    row = lax.broadcasted_iota(jnp.int32, z.shape, 0)
    prev_row = jnp.where(first, 0.0, zp_ref[SUBLANES - 1:SUBLANES, :])
    next_row = jnp.where(last, 0.0, zn_ref[0:1, :])
    z_prev = jnp.where(row == 0, prev_row, pltpu.roll(z, 1, axis=0))
    z_next = jnp.where(row == tm - 1, next_row, pltpu.roll(z, tm - 1, axis=0))
    zs = z + mu_ref[...] * (0.5 * (z_prev + z_next) - z)

    db = ones_ref.shape[0]
    r = zs[:, 0:db]
    k = zs[:, db:2 * db]
    v = zs[:, 2 * db:3 * db]
    o = 3 * db
    zw = zs[:, o:o + LANES]
    za = zs[:, o + LANES:o + 2 * LANES]
    zg = zs[:, o + 2 * LANES:o + 3 * LANES]

    kk = k * kk_w_ref[...]
    kk = kk * lax.rsqrt(_sum_heads(kk * kk, ones_ref[...]) + KK_EPS)
    u = w0_ref[...] + _dot_bf16(jnp.tanh(zw), wup_ref[...])
    nu = -u
    softplus = jnp.maximum(nu, 0.0) + jnp.log(1.0 + jnp.exp(-jnp.abs(nu)))
    decay = jnp.exp(-jnp.exp(-softplus - 0.5))
    a = _sigmoid(a0_ref[...] + _dot_bf16(za, aup_ref[...]))
    ka = ka_ref[...]
    a_f, a_b = a[:, :db], a[:, db:]
    kd_f = k * (1.0 + (a_f - 1.0) * ka)
    kd_b = k * (1.0 + (a_b - 1.0) * ka)
    bonus = _sum_heads(r * (kd_f + kd_b) * rk_ref[...], ones_ref[...]) * v
    gate = _dot_bf16(_sigmoid(zg), gup_ref[...])

    p_ref[:, 0 * db:1 * db] = kk
    p_ref[:, 1 * db:2 * db] = r
    p_ref[:, 2 * db:3 * db] = decay[:, :db]
    p_ref[:, 3 * db:4 * db] = kd_f
    p_ref[:, 4 * db:5 * db] = kk * a_f
    p_ref[:, 5 * db:6 * db] = decay[:, db:]
    p_ref[:, 6 * db:7 * db] = kd_b
    p_ref[:, 7 * db:8 * db] = kk * a_b
    p_ref[:, 8 * db:9 * db] = v
    q_ref[:, 0:db] = bonus
    q_ref[:, db:2 * db] = gate


def _prep(zr, lp, ones_bd, *, tm, n_ctx_tiles, tiles_per_seq):
    n, nr = zr.shape
    db = ones_bd.shape[0]
    hb = tm // SUBLANES
    nblk8 = n // SUBLANES
    row = lambda a: a.reshape(1, -1)
    const = lambda a: pl.BlockSpec(a.shape, lambda i: (0,) * a.ndim)
    params = [row(lp["shift_mu"]), row(lp["k_k"]), row(lp["k_a"]), row(lp["r_k"]),
              row(lp["w0"]), row(lp["a0"]), lp["w_up_bd"], lp["a_up_bd"], lp["g_up"], ones_bd]
    kern = functools.partial(_prep_kernel, tm=tm, n_ctx_tiles=n_ctx_tiles, tiles_per_seq=tiles_per_seq)
    return pl.pallas_call(
        kern,
        grid=(n // tm,),
        in_specs=[
            pl.BlockSpec((tm, nr), lambda i: (i, 0)),
            pl.BlockSpec((SUBLANES, nr), lambda i: (jnp.maximum(i * hb - 1, 0), 0)),
            pl.BlockSpec((SUBLANES, nr), lambda i: (jnp.minimum((i + 1) * hb, nblk8 - 1), 0)),
        ] + [const(a) for a in params],
        out_specs=[
            pl.BlockSpec((tm, N_SCAN_ARRAYS * db), lambda i: (i, 0)),
            pl.BlockSpec((tm, 2 * db), lambda i: (i, 0)),
        ],
        out_shape=[
            jax.ShapeDtypeStruct((n, N_SCAN_ARRAYS * db), F32),
            jax.ShapeDtypeStruct((n, 2 * db), F32),
        ],
        compiler_params=_cparams("arbitrary"),
        name="rwkv_prep",
    )(zr, zr, zr, *params)


def _relayout_kernel(*refs):
    x_refs, o_ref, a_scr = refs[:-2], refs[-2], refs[-1]
    nk = o_ref.shape[0]
    tiles = x_refs[0].shape[1] // LANES
    for b, x_ref in enumerate(x_refs):
        for hp in range(tiles):
            a_scr[pl.ds((b * tiles + hp) * LANES, LANES), :] = x_ref[:, hp * LANES:(hp + 1) * LANES].T
    for k in range(nk):
        o_ref[k] = a_scr[pl.ds(k, LANES, stride=nk), :].T


def _relayout(p, *, bsz, ctx_len, seq):
    n, w_all = p.shape
    width = w_all // N_SCAN_ARRAYS
    heads = width // HEAD
    assert bsz * heads == LANES and ctx_len % LANES == 0 and seq % LANES == 0
    t_all = ctx_len + seq
    ncb = ctx_len // LANES
    nc_blocks = bsz * ncb

    def row_block(b):
        return lambda a, tb: (jnp.where(tb < ncb, b * ncb + tb, nc_blocks + b * (seq // LANES) + tb - ncb), a)

    return pl.pallas_call(
        _relayout_kernel,
        grid=(N_SCAN_ARRAYS, t_all // LANES),
        in_specs=[pl.BlockSpec((LANES, width), row_block(b)) for b in range(bsz)],
        out_specs=pl.BlockSpec((None, HEAD, LANES, LANES), lambda a, tb: (a, 0, tb, 0)),
        out_shape=jax.ShapeDtypeStruct((N_SCAN_ARRAYS, HEAD, t_all, LANES), F32),
        scratch_shapes=[pltpu.VMEM((LANES * HEAD, LANES), F32)],
        compiler_params=_cparams("arbitrary", "arbitrary"),
        name="scan_relayout",
    )(*([p] * bsz))


def _row_bcast(ref, t, k):
    return jnp.broadcast_to(ref[k, pl.ds(t, 1), :], (SUBLANES, LANES))


def _wkv_kernel(kk_ref, r_ref, v_ref, w_ref, kd_ref, b_ref, y_ref, s_scr, sa_scr, *, tc):
    d = pl.program_id(0)
    nk = s_scr.shape[0]
    nvb = s_scr.shape[1] // SUBLANES
    vrows = lambda vb: pl.ds(vb * SUBLANES, SUBLANES)

    @pl.when(pl.program_id(1) == 0)
    def _():
        s_scr[...] = jnp.zeros_like(s_scr)

    t_first = d * (tc - 1)
    acc = [jnp.zeros((SUBLANES, LANES), F32) for _ in range(nvb)]
    for k in range(nk):
        kk0 = _row_bcast(kk_ref, t_first, k)
        for vb in range(nvb):
            acc[vb] = acc[vb] + s_scr[k, vrows(vb), :] * kk0
    for vb in range(nvb):
        sa_scr[vrows(vb), :] = acc[vb]

    sublane = lax.broadcasted_iota(jnp.int32, (SUBLANES, LANES), 0)

    def step(i, carry):
        t = i + d * (tc - 1 - 2 * i)
        tn = jnp.clip(t + 1 - 2 * d, 0, tc - 1)
        sa = [sa_scr[vrows(vb), :] for vb in range(nvb)]
        vv = []
        for vb in range(nvb):
            tile = _row_bcast(v_ref, t, vb * SUBLANES)
            for j in range(1, SUBLANES):
                tile = jnp.where(sublane == j, _row_bcast(v_ref, t, vb * SUBLANES + j), tile)
            vv.append(tile)
        y = [jnp.zeros((SUBLANES, LANES), F32) for _ in range(nvb)]
        san = [jnp.zeros((SUBLANES, LANES), F32) for _ in range(nvb)]
        for k in range(nk):
            wk = _row_bcast(w_ref, t, k)
            bk = _row_bcast(b_ref, t, k)
            kdk = _row_bcast(kd_ref, t, k)
            rk = _row_bcast(r_ref, t, k)
            kkn = _row_bcast(kk_ref, tn, k)
            for vb in range(nvb):
                s = s_scr[k, vrows(vb), :]
                s = s * wk - sa[vb] * bk + vv[vb] * kdk
                s_scr[k, vrows(vb), :] = s
                y[vb] = y[vb] + s * rk
                san[vb] = san[vb] + s * kkn
        for vb in range(nvb):
            y_ref[t, vrows(vb), :] = y[vb]
            sa_scr[vrows(vb), :] = san[vb]
        return carry

    lax.fori_loop(0, tc, step, 0)


def _wkv_scan(p_t, *, ctx_len, tc):
    _, nk, t_all, lanes = p_t.shape
    nb = t_all // tc
    nbc = ctx_len // tc

    def blk(d, j):
        rev = jnp.where(j < nbc, nbc - 1 - j, nb - 1 - j + nbc)
        return jnp.where(d == 0, j, rev)

    shared = lambda idx: pl.BlockSpec((None, nk, tc, lanes), lambda d, j: (idx, 0, blk(d, j), 0))
    per_dir = lambda idx: pl.BlockSpec((None, nk, tc, lanes), lambda d, j: (idx + 3 * d, 0, blk(d, j), 0))
    return pl.pallas_call(
        functools.partial(_wkv_kernel, tc=tc),
        grid=(2, nb),
        in_specs=[shared(0), shared(1), shared(8), per_dir(2), per_dir(3), per_dir(4)],
        out_specs=pl.BlockSpec((None, tc, nk, lanes), lambda d, j: (d, blk(d, j), 0, 0)),
        out_shape=jax.ShapeDtypeStruct((2, t_all, nk, lanes), F32),
        scratch_shapes=[pltpu.VMEM((nk, nk, lanes), F32), pltpu.VMEM((nk, lanes), F32)],
        compiler_params=_cparams("arbitrary", "arbitrary"),
        name="wkv_scan",
    )(p_t, p_t, p_t, p_t, p_t, p_t)


def _mix_out_kernel(x_ref, zc_ref, zcp_ref, zcn_ref, y_ref, q_ref, zg_ref,
                    g1_ref, sh2_ref, sc2_ref, n2_ref, cw_ref, lng_ref, lnb_ref, ones_ref,
                    wa_ref, wb_ref, wo_ref, wr_ref, br_ref,
                    xo_ref, h2_ref, lg_ref, *, tm, n_ctx_tiles, tiles_per_seq):
    i = pl.program_id(0)
    is_ctx = i < n_ctx_tiles
    j = (i - n_ctx_tiles) % tiles_per_seq
    first = jnp.logical_or(is_ctx, j == 0)
    last = jnp.logical_or(is_ctx, j == tiles_per_seq - 1)
    da = ones_ref.shape[0]
    half = da // 2

    zc = zc_ref[...]
    bg, u = zc[:, 0:da], zc[:, da:2 * da] * zc[:, 2 * da:3 * da]
    row = lax.broadcasted_iota(jnp.int32, u.shape, 0)
    period = jnp.where(is_ctx, tm, GRID_W)
    pos = jnp.bitwise_and(row, period - 1)
    u_m1 = jnp.where(pos == 0, 0.0, pltpu.roll(u, 1, axis=0))
    u_p1 = jnp.where(pos == period - 1, 0.0, pltpu.roll(u, tm - 1, axis=0))
    uh = u[:, half:]
    hp = zcp_ref[:, da + half:2 * da] * zcp_ref[:, 2 * da + half:3 * da]
    hn = zcn_ref[:, da + half:2 * da] * zcn_ref[:, 2 * da + half:3 * da]
    hp = jnp.where(first, 0.0, hp)
    hn = jnp.where(last, 0.0, hn)
    uv_m = jnp.concatenate([hp, uh[:tm - GRID_W]], axis=0)
    uv_p = jnp.concatenate([uh[GRID_W:], hn], axis=0)
    um_hi = jnp.where(is_ctx, u_m1[:, half:], uv_m)
    up_hi = jnp.where(is_ctx, u_p1[:, half:], uv_p)
    cw = cw_ref[...]
    conv_lo = u_m1[:, :half] * cw[0:1, :half] + u[:, :half] * cw[1:2, :half] + u_p1[:, :half] * cw[2:3, :half]
    conv_hi = um_hi * cw[0:1, half:] + uh * cw[1:2, half:] + up_hi * cw[2:3, half:]
    ya = (_dot_bf16(bg[:, :half] * conv_lo, wa_ref[0:half, :])
          + _dot_bf16(bg[:, half:] * conv_hi, wa_ref[half:, :]))

    y = y_ref[...]
    inv_n = 1.0 / HEAD
    mean = _sum_heads(y, ones_ref[...]) * inv_n
    yc = y - mean
    var = _sum_heads(yc * yc, ones_ref[...]) * inv_n
    yn = yc * lax.rsqrt(var + GN_EPS) * lng_ref[...] + lnb_ref[...]
    db = y.shape[1]
    yb = _dot_bf16((yn + q_ref[:, 0:db]) * q_ref[:, db:2 * db], wb_ref[...])

    d = x_ref.shape[1]
    merged = _sigmoid(zg_ref[:, 0:d]) * ya + _sigmoid(zg_ref[:, d:2 * d]) * yb
    x_new = x_ref[...] + g1_ref[0] * _dot_bf16(merged, wo_ref[...])
    xo_ref[...] = x_new

    h2 = _rms(x_new, n2_ref[...]) * (1.0 + sc2_ref[0]) + sh2_ref[0]
    h2_ref[...] = h2
    h_hi, h_lo = _split_bf16(h2)
    lg_ref[...] = (jnp.dot(h_hi, wr_ref[0], preferred_element_type=F32)
                   + jnp.dot(h_lo, wr_ref[0], preferred_element_type=F32)
                   + jnp.dot(h_hi, wr_ref[1], preferred_element_type=F32)) + br_ref[...]


def _mix_out(xa, zc, y, q, zg, modl, lp, ones_bd, *, tm, n_ctx_tiles, tiles_per_seq, nc_rows, seq):
    n, d = xa.shape
    da3 = zc.shape[1]
    db = y.shape[1]
    hb = tm // GRID_W
    nblk = n // GRID_W
    seg = functools.partial(_seg_of_tile, tm=tm, nc_rows=nc_rows, seq=seq)
    mod = lambda c: pl.BlockSpec((1, 1, d), lambda i: (seg(i), 0, c))
    const = lambda a: pl.BlockSpec(a.shape, lambda i: (0,) * a.ndim)
    tile = lambda w: pl.BlockSpec((tm, w), lambda i: (i, 0))
    params = [lp["norm2_g"].reshape(1, d), lp["conv_w"], lp["lnx_g"].reshape(1, db), lp["lnx_b"].reshape(1, db),
              ones_bd, lp["w_a_out"], lp["w_b_out"], lp["w_o"], lp["router_w"], lp["router_b"]]
    kern = functools.partial(_mix_out_kernel, tm=tm, n_ctx_tiles=n_ctx_tiles, tiles_per_seq=tiles_per_seq)
    return pl.pallas_call(
        kern,
        grid=(n // tm,),
        in_specs=[
            tile(d), tile(da3),
            pl.BlockSpec((GRID_W, da3), lambda i: (jnp.maximum(i * hb - 1, 0), 0)),
            pl.BlockSpec((GRID_W, da3), lambda i: (jnp.minimum((i + 1) * hb, nblk - 1), 0)),
            tile(db), tile(2 * db), tile(2 * d),
            mod(2), mod(3), mod(4),
        ] + [const(a) for a in params],
        out_specs=[tile(d), tile(d), tile(ROUTER_COLS)],
        out_shape=[
            jax.ShapeDtypeStruct((n, d), F32),
            jax.ShapeDtypeStruct((n, d), F32),
            jax.ShapeDtypeStruct((n, ROUTER_COLS), F32),
        ],
        compiler_params=_cparams("arbitrary"),
        name="mix_out",
    )(xa, zc, zc, zc, y, q, zg, modl, modl, modl, *params)


def _moe_dispatch_kernel(pslot_ref, h_ref, init_hbm, o_hbm, sem):
    del init_hbm
    i = pl.program_id(0)
    tm = h_ref.shape[0]

    def row_copy(r, s):
        return pltpu.make_async_copy(h_ref.at[pl.ds(r, 1)], o_hbm.at[pl.ds(s, 1)], sem.at[0])

    def start(r, carry):
        p = (i * tm + r) * 2
        row_copy(r, pslot_ref[p]).start()
        row_copy(r, pslot_ref[p + 1]).start()
        return carry

    def wait(r, carry):
        row_copy(r, 0).wait()
        row_copy(r, 0).wait()
        return carry

    lax.fori_loop(0, tm, start, 0, unroll=4)
    lax.fori_loop(0, tm, wait, 0, unroll=4)


def _moe_dispatch(h2, pair_slot, slots, *, tm):
    n, d = h2.shape
    n_slots = slots.shape[0]
    return pl.pallas_call(
        _moe_dispatch_kernel,
        grid_spec=pltpu.PrefetchScalarGridSpec(
            num_scalar_prefetch=1,
            grid=(n // tm,),
            in_specs=[pl.BlockSpec((tm, d), lambda i, ps: (i, 0)), pl.BlockSpec(memory_space=pl.ANY)],
            out_specs=pl.BlockSpec(memory_space=pl.ANY),
            scratch_shapes=[pltpu.SemaphoreType.DMA((1,))],
        ),
        out_shape=jax.ShapeDtypeStruct((n_slots, d), F32),
        input_output_aliases={2: 0},
        compiler_params=_cparams("arbitrary"),
        name="moe_dispatch",
    )(pair_slot, h2, slots)


def _start_row_gather(src_hbm, idx_ref, base, stride, buf, slot, sem, n_rows):
    def body(r, carry):
        row = idx_ref[base + r * stride]
        pltpu.make_async_copy(src_hbm.at[pl.ds(row, 1)], buf.at[slot, pl.ds(r, 1)], sem.at[slot]).start()
        return carry
    lax.fori_loop(0, n_rows, body, 0, unroll=8)


def _wait_row_gather(src_hbm, buf, slot, sem, n_rows):
    def body(r, carry):
        pltpu.make_async_copy(src_hbm.at[pl.ds(0, 1)], buf.at[slot, pl.ds(r, 1)], sem.at[slot]).wait()
        return carry
    lax.fori_loop(0, n_rows, body, 0, unroll=8)


def _moe_expert_kernel(bexp_ref, nused_ref, h_ref, w1_ref, w3_ref, w2_ref, o_ref, w1b, w3b, w2b):
    i = pl.program_id(0)
    new_expert = jnp.logical_or(i == 0, bexp_ref[i] != bexp_ref[jnp.maximum(i - 1, 0)])

    @pl.when(new_expert)
    def _():
        w1b[...] = w1_ref[0].astype(BF16)
        w3b[...] = w3_ref[0].astype(BF16)
        w2b[...] = w2_ref[0].astype(BF16)

    @pl.when(i < nused_ref[0])
    def _():
        h = h_ref[...].astype(BF16)
        a = jnp.dot(h, w1b[...], preferred_element_type=F32)
        b = jnp.dot(h, w3b[...], preferred_element_type=F32)
        hid = a * _sigmoid(a) * b
        o_ref[...] = _dot_bf16(hid, w2b[...])

    @pl.when(i >= nused_ref[0])
    def _():
        o_ref[...] = jnp.zeros_like(o_ref)


def _moe_experts(hs, block_exp, n_used, w1, w3, w2):
    n_slots, d = hs.shape
    _, _, f = w1.shape
    blk = MOE_BLOCK
    wspec = lambda shape: pl.BlockSpec((1,) + shape, lambda i, bexp, nused: (bexp[i], 0, 0))
    return pl.pallas_call(
        _moe_expert_kernel,
        grid_spec=pltpu.PrefetchScalarGridSpec(
            num_scalar_prefetch=2,
            grid=(n_slots // blk,),
            in_specs=[
                pl.BlockSpec((blk, d), lambda i, bexp, nused: (i, 0)),
                wspec((d, f)), wspec((d, f)), wspec((f, d)),
            ],
            out_specs=pl.BlockSpec((blk, d), lambda i, bexp, nused: (i, 0)),
            scratch_shapes=[pltpu.VMEM((d, f), BF16), pltpu.VMEM((d, f), BF16), pltpu.VMEM((f, d), BF16)],
        ),
        out_shape=jax.ShapeDtypeStruct((n_slots, d), F32),
        compiler_params=_cparams("arbitrary"),
        name="moe_experts",
    )(block_exp, n_used, hs, w1, w3, w2)


def _moe_combine_kernel(pslot_ref, ys_hbm, x_ref, gate_ref, g2_ref, o_ref, buf_a, buf_b, sem):
    i = pl.program_id(0)
    tm = o_ref.shape[0]
    slot = lax.rem(i, 2)

    def start(tile, s):
        _start_row_gather(ys_hbm, pslot_ref, tile * 2 * tm, 2, buf_a, s, sem, tm)
        _start_row_gather(ys_hbm, pslot_ref, tile * 2 * tm + 1, 2, buf_b, s, sem, tm)

    @pl.when(i == 0)
    def _():
        start(0, 0)

    @pl.when(i + 1 < pl.num_programs(0))
    def _():
        start(i + 1, 1 - slot)

    _wait_row_gather(ys_hbm, buf_a, slot, sem, tm)
    _wait_row_gather(ys_hbm, buf_b, slot, sem, tm)
    gate = gate_ref[...]
    y2 = gate[:, 0:1] * buf_a[slot] + gate[:, 1:2] * buf_b[slot]
    o_ref[...] = x_ref[...] + g2_ref[0] * y2


def _moe_combine(ys, pair_slot, gate, xa, modl, *, tm, nc_rows, seq):
    n, d = xa.shape
    seg = functools.partial(_seg_of_tile, tm=tm, nc_rows=nc_rows, seq=seq)
    return pl.pallas_call(
        _moe_combine_kernel,
        grid_spec=pltpu.PrefetchScalarGridSpec(
            num_scalar_prefetch=1,
            grid=(n // tm,),
            in_specs=[
                pl.BlockSpec(memory_space=pl.ANY),
                pl.BlockSpec((tm, d), lambda i, ps: (i, 0)),
                pl.BlockSpec((tm, 2), lambda i, ps: (i, 0)),
                pl.BlockSpec((1, 1, d), lambda i, ps: (seg(i), 0, 5)),
            ],
            out_specs=pl.BlockSpec((tm, d), lambda i, ps: (i, 0)),
            scratch_shapes=[pltpu.VMEM((2, tm, d), F32), pltpu.VMEM((2, tm, d), F32), pltpu.SemaphoreType.DMA((2,))],
        ),
        out_shape=jax.ShapeDtypeStruct((n, d), F32),
        compiler_params=_cparams("arbitrary"),
        name="moe_combine",
    )(pair_slot, ys, xa, gate, modl)


def _route(logits):
    lg = logits[:, :N_GROUPS]
    le = logits[:, N_GROUPS:N_GROUPS + N_EXPERTS].reshape(-1, N_GROUPS, EXP_PER_GROUP)
    p_g = jax.nn.softmax(lg, axis=-1)
    g_sel = jnp.argmax(lg, axis=-1).astype(jnp.int32)
    le = jnp.take_along_axis(le, g_sel[:, None, None], axis=1)[:, 0]
    top_p, top_i = lax.top_k(jax.nn.softmax(le, axis=-1), 2)
    gate = jnp.max(p_g, axis=-1, keepdims=True) * top_p / jnp.sum(top_p, axis=-1, keepdims=True)
    expert = (g_sel[:, None] * EXP_PER_GROUP + top_i).astype(jnp.int32)
    return expert, gate


def _dispatch(expert, blk):
    m = expert.size
    e_flat = expert.reshape(m)
    onehot = (e_flat[:, None] == jnp.arange(N_EXPERTS, dtype=jnp.int32)[None, :]).astype(BF16)
    onehot = onehot.reshape(m // LANES, LANES, N_EXPERTS)
    tril = jnp.tril(jnp.ones((LANES, LANES), BF16))
    within = jnp.einsum("ij,bjk->bik", tril, onehot, preferred_element_type=F32)
    chunk_tot = within[:, -1, :]
    chunk_off = jnp.cumsum(chunk_tot, axis=0) - chunk_tot
    counts = (chunk_off[-1] + chunk_tot[-1]).astype(jnp.int32)
    rank = jnp.sum((within + chunk_off[:, None, :]) * onehot.astype(F32), axis=-1).reshape(m).astype(jnp.int32) - 1
    padded = (counts + blk - 1) // blk * blk
    pad_end = jnp.cumsum(padded)
    dest = ((pad_end - padded)[e_flat] + rank).astype(jnp.int32)
    n_blocks = -(-m // blk) + N_EXPERTS
    block_exp = jnp.minimum(jnp.searchsorted(pad_end, jnp.arange(n_blocks, dtype=jnp.int32) * blk, side="right"),
                            N_EXPERTS - 1).astype(jnp.int32)
    n_used = (pad_end[-1] // blk).astype(jnp.int32).reshape(1)
    return dest, block_exp, n_used


def _final_norm_kernel(x_ref, g_ref, o_ref):
    o_ref[...] = _rms(x_ref[...], g_ref[...])


def _final_norm(xa, g, *, tm, row_off, n_out):
    d = xa.shape[1]
    off = row_off // tm
    return pl.pallas_call(
        _final_norm_kernel,
        grid=(n_out // tm,),
        in_specs=[pl.BlockSpec((tm, d), lambda i: (i + off, 0)), pl.BlockSpec((1, d), lambda i: (0, 0))],
        out_specs=pl.BlockSpec((tm, d), lambda i: (i, 0)),
        out_shape=jax.ShapeDtypeStruct((n_out, d), F32),
        compiler_params=_cparams("arbitrary"),
        name="final_norm",
    )(xa, g.reshape(1, d))


def _block_diag2(m):
    z = jnp.zeros_like(m[0])
    return jnp.concatenate([jnp.concatenate([m[0], z], axis=1), jnp.concatenate([z, m[1]], axis=1)], axis=0)


def kernel(x, c, ctx, c_ctx, w_mod, b_mod, norm1_g, norm2_g, w_in, shift_mu, conv_w, w_up, w0, a_up, a0, g_up, k_k, k_a, r_k, lnx_g, lnx_b, w_a_out, w_b_out, w_o, router_g, router_g_b, router_e, router_e_b, exp_w1, exp_w3, exp_w2, final_g):
    bsz, seq, d = x.shape
    ctx_len = ctx.shape[1]
    depth = w_mod.shape[0]
    da = w_a_out.shape[1]
    db = w_b_out.shape[1]
    heads = db // HEAD
    assert bsz * heads == LANES and da == db and seq % ctx_len == 0 and ctx_len % GRID_W == 0
    assert bsz + 1 <= MOD_ROWS
    nc_rows = bsz * ctx_len
    n_all = nc_rows + bsz * seq
    t_all = ctx_len + seq
    tm = ctx_len
    tm_mm = min(512, seq)
    tc = min(SCAN_STEPS, ctx_len)
    tiles = dict(tm=tm, n_ctx_tiles=bsz, tiles_per_seq=seq // tm)
    segs = dict(nc_rows=nc_rows, seq=seq)

    s_all = jnp.zeros((MOD_ROWS, d), F32).at[0].set(c_ctx).at[1:1 + bsz].set(c)
    mod = _modulation(s_all, w_mod, b_mod)
    xa = jnp.concatenate([ctx.reshape(nc_rows, d), x.reshape(bsz * seq, d)], axis=0)
    ones_bd = jnp.kron(jnp.eye(heads, dtype=F32), jnp.ones((HEAD, HEAD), F32)).astype(BF16)
    rw0 = 3 * da
    rw1 = rw0 + 3 * db + 3 * LANES
    n_slots = (-(-2 * n_all // MOE_BLOCK) + N_EXPERTS) * MOE_BLOCK
    hs = jnp.zeros((n_slots, d), F32)

    for l in range(depth):
        modl = mod[l].reshape(MOD_ROWS, 1, 6 * d)
        w_in_b = w_in[l].astype(BF16)
        router_w = jnp.zeros((d, ROUTER_COLS), F32)
        router_w = router_w.at[:, :N_GROUPS].set(router_g[l]).at[:, N_GROUPS:N_GROUPS + N_EXPERTS].set(router_e[l])
        router_b = jnp.zeros((1, ROUTER_COLS), F32)
        router_b = router_b.at[0, :N_GROUPS].set(router_g_b[l]).at[0, N_GROUPS:N_GROUPS + N_EXPERTS].set(router_e_b[l])
        router_hi = router_w.astype(BF16)
        router_lo = (router_w - router_hi.astype(F32)).astype(BF16)
        lp = {
            "shift_mu": shift_mu[l], "k_k": k_k[l], "k_a": k_a[l], "r_k": r_k[l],
            "w0": w0[l], "a0": a0[l],
            "w_up_bd": _block_diag2(w_up[l]).astype(BF16), "a_up_bd": _block_diag2(a_up[l]).astype(BF16),
            "g_up": g_up[l].astype(BF16), "norm2_g": norm2_g[l], "conv_w": conv_w[l],
            "lnx_g": lnx_g[l], "lnx_b": lnx_b[l],
            "w_a_out": w_a_out[l].astype(BF16), "w_b_out": w_b_out[l].astype(BF16), "w_o": w_o[l].astype(BF16),
            "router_w": jnp.stack([router_hi, router_lo]), "router_b": router_b,
        }
        zc, zr, zg = _in_proj(xa, norm1_g[l].reshape(1, d), modl, w_in_b[:, :rw0], w_in_b[:, rw0:rw1],
                              w_in_b[:, rw1:], tm=tm_mm, **segs)
        p, q = _prep(zr, lp, ones_bd, **tiles)
        p_t = _relayout(p, bsz=bsz, ctx_len=ctx_len, seq=seq)
        y_t = _wkv_scan(p_t, ctx_len=ctx_len, tc=tc)
        y_bt = (y_t[0] + y_t[1]).reshape(t_all, HEAD, bsz, heads).transpose(2, 0, 3, 1)
        y = jnp.concatenate([y_bt[:, :ctx_len].reshape(nc_rows, db), y_bt[:, ctx_len:].reshape(bsz * seq, db)], axis=0)
        xa, h2, logits = _mix_out(xa, zc, y, q, zg, modl, lp, ones_bd, **tiles, **segs)
        expert, gate = _route(logits)
        pair_slot, block_exp, n_used = _dispatch(expert, MOE_BLOCK)
        hs = _moe_dispatch(h2, pair_slot, hs, tm=tm)
        ys = _moe_experts(hs, block_exp, n_used, exp_w1[l], exp_w3[l], exp_w2[l])
        xa = _moe_combine(ys, pair_slot, gate, xa, modl, tm=tm, **segs)

    out = _final_norm(xa, final_g, tm=tm, row_off=nc_rows, n_out=bsz * seq)
    return out.reshape(bsz, seq, d)
```

```python
import functools

import jax
import jax.numpy as jnp
from jax import lax
from jax.experimental import pallas as pl
from jax.experimental.pallas import tpu as pltpu

F32 = jnp.float32
BF16 = jnp.bfloat16
HIGHEST = lax.Precision.HIGHEST

NORM_EPS = 1e-6
GN_EPS = 64e-5
KK_EPS = 1e-12

GRID_W = 64
HEAD = 64
N_GROUPS = 4
EXP_PER_GROUP = 8
N_EXPERTS = N_GROUPS * EXP_PER_GROUP
SUBLANES = 8
LANES = 128
MOD_ROWS = 24
ROUTER_COLS = LANES
SCAN_STEPS = 32
MOE_BLOCK = 256
N_SCAN_ARRAYS = 9
VMEM_LIMIT = 56 * 1024 * 1024


def _cparams(*sem):
    return pltpu.CompilerParams(dimension_semantics=sem, vmem_limit_bytes=VMEM_LIMIT)


def _sigmoid(x):
    return 1.0 / (1.0 + jnp.exp(-x))


def _dot_hi(a, b):
    return jnp.dot(a, b, precision=HIGHEST, preferred_element_type=F32)


def _dot_bf16(a, b):
    return jnp.dot(a.astype(BF16), b, preferred_element_type=F32)


def _split_bf16(x):
    hi = x.astype(BF16)
    lo = (x - hi.astype(F32)).astype(BF16)
    return hi, lo


def _sum_heads(x, ones_bd):
    hi, lo = _split_bf16(x)
    return (jnp.dot(hi, ones_bd, preferred_element_type=F32)
            + jnp.dot(lo, ones_bd, preferred_element_type=F32))


def _rms(x, g):
    ms = jnp.mean(x * x, axis=-1, keepdims=True)
    return x * lax.rsqrt(ms + NORM_EPS) * g


def _mod_kernel(s_ref, w_ref, b_ref, o_ref):
    s = s_ref[...]
    s = s * _sigmoid(s)
    o_ref[0] = _dot_hi(s, w_ref[0]) + b_ref[0]


def _modulation(s_all, w_mod, b_mod):
    depth, d, n6 = w_mod.shape
    tn = d
    return pl.pallas_call(
        _mod_kernel,
        grid=(depth, n6 // tn),
        in_specs=[
            pl.BlockSpec((MOD_ROWS, d), lambda l, j: (0, 0)),
            pl.BlockSpec((1, d, tn), lambda l, j: (l, 0, j)),
            pl.BlockSpec((1, 1, tn), lambda l, j: (l, 0, j)),
        ],
        out_specs=pl.BlockSpec((1, MOD_ROWS, tn), lambda l, j: (l, 0, j)),
        out_shape=jax.ShapeDtypeStruct((depth, MOD_ROWS, n6), F32),
        compiler_params=_cparams("arbitrary", "arbitrary"),
        name="modulation",
    )(s_all, w_mod, b_mod.reshape(depth, 1, n6))


def _seg_of_tile(i, tm, nc_rows, seq):
    return jnp.where(i * tm < nc_rows, 0, 1 + (i * tm - nc_rows) // seq)


def _in_proj_kernel(x_ref, g_ref, sh_ref, sc_ref, wc_ref, wr_ref, wg_ref, zc_ref, zr_ref, zg_ref):
    h = _rms(x_ref[...], g_ref[...]) * (1.0 + sc_ref[0]) + sh_ref[0]
    hb = h.astype(BF16)
    zc_ref[...] = jnp.dot(hb, wc_ref[...], preferred_element_type=F32)
    zr_ref[...] = jnp.dot(hb, wr_ref[...], preferred_element_type=F32)
    zg_ref[...] = jnp.dot(hb, wg_ref[...], preferred_element_type=F32)


def _in_proj(xa, norm_g, modl, w_conv, w_rw, w_gate, *, tm, nc_rows, seq):
    n, d = xa.shape
    seg = functools.partial(_seg_of_tile, tm=tm, nc_rows=nc_rows, seq=seq)
    const = lambda shape: pl.BlockSpec(shape, lambda i: (0, 0), pipeline_mode=pl.Buffered(1))
    nc_, nr_, ng_ = w_conv.shape[1], w_rw.shape[1], w_gate.shape[1]
    return pl.pallas_call(
        _in_proj_kernel,
        grid=(n // tm,),
        in_specs=[
            pl.BlockSpec((tm, d), lambda i: (i, 0)),
            pl.BlockSpec((1, d), lambda i: (0, 0)),
            pl.BlockSpec((1, 1, d), lambda i: (seg(i), 0, 0)),
            pl.BlockSpec((1, 1, d), lambda i: (seg(i), 0, 1)),
            const((d, nc_)), const((d, nr_)), const((d, ng_)),
        ],
        out_specs=[
            pl.BlockSpec((tm, nc_), lambda i: (i, 0)),
            pl.BlockSpec((tm, nr_), lambda i: (i, 0)),
            pl.BlockSpec((tm, ng_), lambda i: (i, 0)),
        ],
        out_shape=[
            jax.ShapeDtypeStruct((n, nc_), F32),
            jax.ShapeDtypeStruct((n, nr_), F32),
            jax.ShapeDtypeStruct((n, ng_), F32),
        ],
        compiler_params=_cparams("arbitrary"),
        name="in_proj",
    )(xa, norm_g, modl, modl, w_conv, w_rw, w_gate)


def _prep_kernel(z_ref, zp_ref, zn_ref, mu_ref, kk_w_ref, ka_ref, rk_ref, w0_ref, a0_ref,
                 wup_ref, aup_ref, gup_ref, ones_ref, p_ref, q_ref, *, tm, n_ctx_tiles, tiles_per_seq):
    i = pl.program_id(0)
    is_ctx = i < n_ctx_tiles
    j = (i - n_ctx_tiles) % tiles_per_seq
    first = jnp.logical_or(is_ctx, j == 0)
    last = jnp.logical_or(is_ctx, j == tiles_per_seq - 1)
    z = z_ref[...]
    row = lax.broadcasted_iota(jnp.int32, z.shape, 0)
    prev_row = jnp.where(first, 0.0, zp_ref[SUBLANES - 1:SUBLANES, :])
    next_row = jnp.where(last, 0.0, zn_ref[0:1, :])
    z_prev = jnp.where(row == 0, prev_row, pltpu.roll(z, 1, axis=0))
    z_next = jnp.where(row == tm - 1, next_row, pltpu.roll(z, tm - 1, axis=0))
    zs = z + mu_ref[...] * (0.5 * (z_prev + z_next) - z)

    db = ones_ref.shape[0]
    r = zs[:, 0:db]
    k = zs[:, db:2 * db]
    v = zs[:, 2 * db:3 * db]
    o = 3 * db
    zw = zs[:, o:o + LANES]
    za = zs[:, o + LANES:o + 2 * LANES]
    zg = zs[:, o + 2 * LANES:o + 3 * LANES]

    kk = k * kk_w_ref[...]
    kk = kk * lax.rsqrt(_sum_heads(kk * kk, ones_ref[...]) + KK_EPS)
    u = w0_ref[...] + _dot_bf16(jnp.tanh(zw), wup_ref[...])
    nu = -u
    softplus = jnp.maximum(nu, 0.0) + jnp.log(1.0 + jnp.exp(-jnp.abs(nu)))
    decay = jnp.exp(-jnp.exp(-softplus - 0.5))
    a = _sigmoid(a0_ref[...] + _dot_bf16(za, aup_ref[...]))
    ka = ka_ref[...]
    a_f, a_b = a[:, :db], a[:, db:]
    kd_f = k * (1.0 + (a_f - 1.0) * ka)
    kd_b = k * (1.0 + (a_b - 1.0) * ka)
    bonus = _sum_heads(r * (kd_f + kd_b) * rk_ref[...], ones_ref[...]) * v
    gate = _dot_bf16(_sigmoid(zg), gup_ref[...])

    p_ref[:, 0 * db:1 * db] = kk
    p_ref[:, 1 * db:2 * db] = r
    p_ref[:, 2 * db:3 * db] = decay[:, :db]
    p_ref[:, 3 * db:4 * db] = kd_f
    p_ref[:, 4 * db:5 * db] = kk * a_f
    p_ref[:, 5 * db:6 * db] = decay[:, db:]
    p_ref[:, 6 * db:7 * db] = kd_b
    p_ref[:, 7 * db:8 * db] = kk * a_b
    p_ref[:, 8 * db:9 * db] = v
    q_ref[:, 0:db] = bonus
    q_ref[:, db:2 * db] = gate


def _prep(zr, lp, ones_bd, *, tm, n_ctx_tiles, tiles_per_seq):
    n, nr = zr.shape
    db = ones_bd.shape[0]
    hb = tm // SUBLANES
    nblk8 = n // SUBLANES
    row = lambda a: a.reshape(1, -1)
    const = lambda a: pl.BlockSpec(a.shape, lambda i: (0,) * a.ndim)
    params = [row(lp["shift_mu"]), row(lp["k_k"]), row(lp["k_a"]), row(lp["r_k"]),
              row(lp["w0"]), row(lp["a0"]), lp["w_up_bd"], lp["a_up_bd"], lp["g_up"], ones_bd]
    kern = functools.partial(_prep_kernel, tm=tm, n_ctx_tiles=n_ctx_tiles, tiles_per_seq=tiles_per_seq)
    return pl.pallas_call(
        kern,
        grid=(n // tm,),
        in_specs=[
            pl.BlockSpec((tm, nr), lambda i: (i, 0)),
            pl.BlockSpec((SUBLANES, nr), lambda i: (jnp.maximum(i * hb - 1, 0), 0)),
            pl.BlockSpec((SUBLANES, nr), lambda i: (jnp.minimum((i + 1) * hb, nblk8 - 1), 0)),
        ] + [const(a) for a in params],
        out_specs=[
            pl.BlockSpec((tm, N_SCAN_ARRAYS * db), lambda i: (i, 0)),
            pl.BlockSpec((tm, 2 * db), lambda i: (i, 0)),
        ],
        out_shape=[
            jax.ShapeDtypeStruct((n, N_SCAN_ARRAYS * db), F32),
            jax.ShapeDtypeStruct((n, 2 * db), F32),
        ],
        compiler_params=_cparams("arbitrary"),
        name="rwkv_prep",
    )(zr, zr, zr, *params)


def _relayout_kernel(*refs):
    x_refs, o_ref, a_scr = refs[:-2], refs[-2], refs[-1]
    nk = o_ref.shape[0]
    tiles = x_refs[0].shape[1] // LANES
    for b, x_ref in enumerate(x_refs):
        for hp in range(tiles):
            a_scr[pl.ds((b * tiles + hp) * LANES, LANES), :] = x_ref[:, hp * LANES:(hp + 1) * LANES].T
    for k in range(nk):
        o_ref[k] = a_scr[pl.ds(k, LANES, stride=nk), :].T


def _relayout(p, *, bsz, ctx_len, seq):
    n, w_all = p.shape
    width = w_all // N_SCAN_ARRAYS
    heads = width // HEAD
    assert bsz * heads == LANES and ctx_len % LANES == 0 and seq % LANES == 0
    t_all = ctx_len + seq
    ncb = ctx_len // LANES
    nc_blocks = bsz * ncb

    def row_block(b):
        return lambda a, tb: (jnp.where(tb < ncb, b * ncb + tb, nc_blocks + b * (seq // LANES) + tb - ncb), a)

    return pl.pallas_call(
        _relayout_kernel,
        grid=(N_SCAN_ARRAYS, t_all // LANES),
        in_specs=[pl.BlockSpec((LANES, width), row_block(b)) for b in range(bsz)],
        out_specs=pl.BlockSpec((None, HEAD, LANES, LANES), lambda a, tb: (a, 0, tb, 0)),
        out_shape=jax.ShapeDtypeStruct((N_SCAN_ARRAYS, HEAD, t_all, LANES), F32),
        scratch_shapes=[pltpu.VMEM((LANES * HEAD, LANES), F32)],
        compiler_params=_cparams("arbitrary", "arbitrary"),
        name="scan_relayout",
    )(*([p] * bsz))


def _row_bcast(ref, t, k):
    return jnp.broadcast_to(ref[k, pl.ds(t, 1), :], (SUBLANES, LANES))


def _wkv_kernel(kk_ref, r_ref, v_ref, w_ref, kd_ref, b_ref, y_ref, s_scr, sa_scr, *, tc):
    d = pl.program_id(0)
    nk = s_scr.shape[0]
    nvb = s_scr.shape[1] // SUBLANES
    vrows = lambda vb: pl.ds(vb * SUBLANES, SUBLANES)

    @pl.when(pl.program_id(1) == 0)
    def _():
        s_scr[...] = jnp.zeros_like(s_scr)

    t_first = d * (tc - 1)
    acc = [jnp.zeros((SUBLANES, LANES), F32) for _ in range(nvb)]
    for k in range(nk):
        kk0 = _row_bcast(kk_ref, t_first, k)
        for vb in range(nvb):
            acc[vb] = acc[vb] + s_scr[k, vrows(vb), :] * kk0
    for vb in range(nvb):
        sa_scr[vrows(vb), :] = acc[vb]

    sublane = lax.broadcasted_iota(jnp.int32, (SUBLANES, LANES), 0)

    def step(i, carry):
        t = i + d * (tc - 1 - 2 * i)
        tn = jnp.clip(t + 1 - 2 * d, 0, tc - 1)
        sa = [sa_scr[vrows(vb), :] for vb in range(nvb)]
        vv = []
        for vb in range(nvb):
            tile = _row_bcast(v_ref, t, vb * SUBLANES)
            for j in range(1, SUBLANES):
                tile = jnp.where(sublane == j, _row_bcast(v_ref, t, vb * SUBLANES + j), tile)
            vv.append(tile)
        y = [jnp.zeros((SUBLANES, LANES), F32) for _ in range(nvb)]
        san = [jnp.zeros((SUBLANES, LANES), F32) for _ in range(nvb)]
        for k in range(nk):
            wk = _row_bcast(w_ref, t, k)
            bk = _row_bcast(b_ref, t, k)
            kdk = _row_bcast(kd_ref, t, k)
            rk = _row_bcast(r_ref, t, k)
            kkn = _row_bcast(kk_ref, tn, k)
            for vb in range(nvb):
                s = s_scr[k, vrows(vb), :]
                s = s * wk - sa[vb] * bk + vv[vb] * kdk
                s_scr[k, vrows(vb), :] = s
                y[vb] = y[vb] + s * rk
                san[vb] = san[vb] + s * kkn
        for vb in range(nvb):
            y_ref[t, vrows(vb), :] = y[vb]
            sa_scr[vrows(vb), :] = san[vb]
        return carry

    lax.fori_loop(0, tc, step, 0)


def _wkv_scan(p_t, *, ctx_len, tc):
    _, nk, t_all, lanes = p_t.shape
    nb = t_all // tc
    nbc = ctx_len // tc

    def blk(d, j):
        rev = jnp.where(j < nbc, nbc - 1 - j, nb - 1 - j + nbc)
        return jnp.where(d == 0, j, rev)

    shared = lambda idx: pl.BlockSpec((None, nk, tc, lanes), lambda d, j: (idx, 0, blk(d, j), 0))
    per_dir = lambda idx: pl.BlockSpec((None, nk, tc, lanes), lambda d, j: (idx + 3 * d, 0, blk(d, j), 0))
    return pl.pallas_call(
        functools.partial(_wkv_kernel, tc=tc),
        grid=(2, nb),
        in_specs=[shared(0), shared(1), shared(8), per_dir(2), per_dir(3), per_dir(4)],
        out_specs=pl.BlockSpec((None, tc, nk, lanes), lambda d, j: (d, blk(d, j), 0, 0)),
        out_shape=jax.ShapeDtypeStruct((2, t_all, nk, lanes), F32),
        scratch_shapes=[pltpu.VMEM((nk, nk, lanes), F32), pltpu.VMEM((nk, lanes), F32)],
        compiler_params=_cparams("arbitrary", "arbitrary"),
        name="wkv_scan",
    )(p_t, p_t, p_t, p_t, p_t, p_t)


def _mix_out_kernel(x_ref, zc_ref, zcp_ref, zcn_ref, y_ref, q_ref, zg_ref,
                    g1_ref, sh2_ref, sc2_ref, n2_ref, cw_ref, lng_ref, lnb_ref, ones_ref,
                    wa_ref, wb_ref, wo_ref, wr_ref, br_ref,
                    xo_ref, h2_ref, lg_ref, *, tm, n_ctx_tiles, tiles_per_seq):
    i = pl.program_id(0)
    is_ctx = i < n_ctx_tiles
    j = (i - n_ctx_tiles) % tiles_per_seq
    first = jnp.logical_or(is_ctx, j == 0)
    last = jnp.logical_or(is_ctx, j == tiles_per_seq - 1)
    da = ones_ref.shape[0]
    half = da // 2

    zc = zc_ref[...]
    bg, u = zc[:, 0:da], zc[:, da:2 * da] * zc[:, 2 * da:3 * da]
    row = lax.broadcasted_iota(jnp.int32, u.shape, 0)
    period = jnp.where(is_ctx, tm, GRID_W)
    pos = jnp.bitwise_and(row, period - 1)
    u_m1 = jnp.where(pos == 0, 0.0, pltpu.roll(u, 1, axis=0))
    u_p1 = jnp.where(pos == period - 1, 0.0, pltpu.roll(u, tm - 1, axis=0))
    uh = u[:, half:]
    hp = zcp_ref[:, da + half:2 * da] * zcp_ref[:, 2 * da + half:3 * da]
    hn = zcn_ref[:, da + half:2 * da] * zcn_ref[:, 2 * da + half:3 * da]
    hp = jnp.where(first, 0.0, hp)
    hn = jnp.where(last, 0.0, hn)
    uv_m = jnp.concatenate([hp, uh[:tm - GRID_W]], axis=0)
    uv_p = jnp.concatenate([uh[GRID_W:], hn], axis=0)
    um_hi = jnp.where(is_ctx, u_m1[:, half:], uv_m)
    up_hi = jnp.where(is_ctx, u_p1[:, half:], uv_p)
    cw = cw_ref[...]
    conv_lo = u_m1[:, :half] * cw[0:1, :half] + u[:, :half] * cw[1:2, :half] + u_p1[:, :half] * cw[2:3, :half]
    conv_hi = um_hi * cw[0:1, half:] + uh * cw[1:2, half:] + up_hi * cw[2:3, half:]
    ya = (_dot_bf16(bg[:, :half] * conv_lo, wa_ref[0:half, :])
          + _dot_bf16(bg[:, half:] * conv_hi, wa_ref[half:, :]))

    y = y_ref[...]
    inv_n = 1.0 / HEAD
    mean = _sum_heads(y, ones_ref[...]) * inv_n
    yc = y - mean
    var = _sum_heads(yc * yc, ones_ref[...]) * inv_n
    yn = yc * lax.rsqrt(var + GN_EPS) * lng_ref[...] + lnb_ref[...]
    db = y.shape[1]
    yb = _dot_bf16((yn + q_ref[:, 0:db]) * q_ref[:, db:2 * db], wb_ref[...])

    d = x_ref.shape[1]
    merged = _sigmoid(zg_ref[:, 0:d]) * ya + _sigmoid(zg_ref[:, d:2 * d]) * yb
    x_new = x_ref[...] + g1_ref[0] * _dot_bf16(merged, wo_ref[...])
    xo_ref[...] = x_new

    h2 = _rms(x_new, n2_ref[...]) * (1.0 + sc2_ref[0]) + sh2_ref[0]
    h2_ref[...] = h2
    h_hi, h_lo = _split_bf16(h2)
    lg_ref[...] = (jnp.dot(h_hi, wr_ref[0], preferred_element_type=F32)
                   + jnp.dot(h_lo, wr_ref[0], preferred_element_type=F32)
                   + jnp.dot(h_hi, wr_ref[1], preferred_element_type=F32)) + br_ref[...]


def _mix_out(xa, zc, y, q, zg, modl, lp, ones_bd, *, tm, n_ctx_tiles, tiles_per_seq, nc_rows, seq):
    n, d = xa.shape
    da3 = zc.shape[1]
    db = y.shape[1]
    hb = tm // GRID_W
    nblk = n // GRID_W
    seg = functools.partial(_seg_of_tile, tm=tm, nc_rows=nc_rows, seq=seq)
    mod = lambda c: pl.BlockSpec((1, 1, d), lambda i: (seg(i), 0, c))
    const = lambda a: pl.BlockSpec(a.shape, lambda i: (0,) * a.ndim)
    tile = lambda w: pl.BlockSpec((tm, w), lambda i: (i, 0))
    params = [lp["norm2_g"].reshape(1, d), lp["conv_w"], lp["lnx_g"].reshape(1, db), lp["lnx_b"].reshape(1, db),
              ones_bd, lp["w_a_out"], lp["w_b_out"], lp["w_o"], lp["router_w"], lp["router_b"]]
    kern = functools.partial(_mix_out_kernel, tm=tm, n_ctx_tiles=n_ctx_tiles, tiles_per_seq=tiles_per_seq)
    return pl.pallas_call(
        kern,
        grid=(n // tm,),
        in_specs=[
            tile(d), tile(da3),
            pl.BlockSpec((GRID_W, da3), lambda i: (jnp.maximum(i * hb - 1, 0), 0)),
            pl.BlockSpec((GRID_W, da3), lambda i: (jnp.minimum((i + 1) * hb, nblk - 1), 0)),
            tile(db), tile(2 * db), tile(2 * d),
            mod(2), mod(3), mod(4),
        ] + [const(a) for a in params],
        out_specs=[tile(d), tile(d), tile(ROUTER_COLS)],
        out_shape=[
            jax.ShapeDtypeStruct((n, d), F32),
            jax.ShapeDtypeStruct((n, d), F32),
            jax.ShapeDtypeStruct((n, ROUTER_COLS), F32),
        ],
        compiler_params=_cparams("arbitrary"),
        name="mix_out",
    )(xa, zc, zc, zc, y, q, zg, modl, modl, modl, *params)


def _moe_fused_kernel(bexp_ref, nused_ref, spair_ref, h_hbm, w1_ref, w3_ref, w2_ref, yp_hbm,
                      hbuf, obuf, w1b, w3b, w2b, gsem, ssem, *, n_tok):
    i = pl.program_id(0)
    n_used = nused_ref[0]
    n_blocks = pl.num_programs(0)
    blk = hbuf.shape[1]

    def gather(block, slot, r):
        tok = jnp.minimum(lax.shift_right_logical(spair_ref[block * blk + r], 1), n_tok - 1)
        return pltpu.make_async_copy(h_hbm.at[pl.ds(tok, 1)], hbuf.at[slot, pl.ds(r, 1)], gsem.at[slot])

    def scatter(block, slot, r):
        dst = spair_ref[block * blk + r]
        return pltpu.make_async_copy(obuf.at[slot, pl.ds(r, 1)], yp_hbm.at[pl.ds(dst, 1)], ssem.at[slot])

    def wait_gather(slot):
        for r in range(blk):
            pltpu.make_async_copy(h_hbm.at[pl.ds(0, 1)], hbuf.at[slot, pl.ds(r, 1)], gsem.at[slot]).wait()

    def wait_scatter(slot):
        for r in range(blk):
            pltpu.make_async_copy(obuf.at[slot, pl.ds(r, 1)], yp_hbm.at[pl.ds(0, 1)], ssem.at[slot]).wait()

    @pl.when(i == 0)
    def _():
        for r in range(blk):
            gather(0, 0, r).start()
        obuf[1] = jnp.zeros(obuf.shape[1:], obuf.dtype)

    @pl.when(jnp.logical_or(i == 0, bexp_ref[i] != bexp_ref[jnp.maximum(i - 1, 0)]))
    def _():
        w1b[...] = w1_ref[0].astype(BF16)
        w3b[...] = w3_ref[0].astype(BF16)
        w2b[...] = w2_ref[0].astype(BF16)

    def run_block(s):
        wait_gather(s)

        @pl.when(i >= 1)
        def _():
            wait_scatter(s)

        nxt = jnp.minimum(i + 1, n_blocks - 1)
        prev = jnp.maximum(i - 1, 0)
        for r in range(blk):
            gather(nxt, 1 - s, r).start()
            scatter(prev, 1 - s, r).start()
        h = hbuf[s].astype(BF16)
        a = jnp.dot(h, w1b[...], preferred_element_type=F32)
        b = jnp.dot(h, w3b[...], preferred_element_type=F32)
        hid = a * _sigmoid(a) * b
        obuf[s] = _dot_bf16(hid, w2b[...])

    for s in range(2):
        @pl.when(jnp.logical_and(i < n_used, lax.rem(i, 2) == s))
        def _(s=s):
            run_block(s)

    for s in range(2):
        @pl.when(jnp.logical_and(i == n_used, lax.rem(i, 2) == s))
        def _(s=s):
            wait_gather(s)
            wait_scatter(s)
            for r in range(blk):
                scatter(i - 1, 1 - s, r).start()
            wait_scatter(1 - s)


def _moe_fused(h2, slot_pair, block_exp, n_used, w1_all, w3_all, w2_all, layer):
    n, d = h2.shape
    f = w1_all.shape[-1]
    blk = MOE_BLOCK
    n_slots = slot_pair.shape[0]
    wspec = lambda shape: pl.BlockSpec((None, 1) + shape, lambda i, bexp, nused, sp: (layer, bexp[i], 0, 0))
    return pl.pallas_call(
        functools.partial(_moe_fused_kernel, n_tok=n),
        grid_spec=pltpu.PrefetchScalarGridSpec(
            num_scalar_prefetch=3,
            grid=(n_slots // blk,),
            in_specs=[pl.BlockSpec(memory_space=pl.ANY), wspec((d, f)), wspec((d, f)), wspec((f, d))],
            out_specs=pl.BlockSpec(memory_space=pl.ANY),
            scratch_shapes=[
                pltpu.VMEM((2, blk, d), F32), pltpu.VMEM((2, blk, d), F32),
                pltpu.VMEM((d, f), BF16), pltpu.VMEM((d, f), BF16), pltpu.VMEM((f, d), BF16),
                pltpu.SemaphoreType.DMA((2,)), pltpu.SemaphoreType.DMA((2,)),
            ],
        ),
        out_shape=jax.ShapeDtypeStruct((2 * n + blk, d), F32),
        compiler_params=_cparams("arbitrary"),
        name="moe_experts",
    )(block_exp, n_used, slot_pair, h2, w1_all, w3_all, w2_all)


def _moe_mix_kernel(yp_ref, x_ref, gate_ref, g2_ref, fg_ref, o_ref, *, final):
    d = o_ref.shape[1]
    gate = gate_ref[...]
    y2 = gate[:, 0:1] * yp_ref[:, 0:d] + gate[:, 1:2] * yp_ref[:, d:2 * d]
    x_new = x_ref[...] + g2_ref[0] * y2
    o_ref[...] = _rms(x_new, fg_ref[...]) if final else x_new


def _moe_mix(yp, gate, xa, modl, final_g, *, final, tm, nc_rows, seq):
    n, d = xa.shape
    seg = functools.partial(_seg_of_tile, tm=tm, nc_rows=nc_rows, seq=seq)
    return pl.pallas_call(
        functools.partial(_moe_mix_kernel, final=final),
        grid=(n // tm,),
        in_specs=[
            pl.BlockSpec((tm, 2 * d), lambda i: (i, 0)),
            pl.BlockSpec((tm, d), lambda i: (i, 0)),
            pl.BlockSpec((tm, 2), lambda i: (i, 0)),
            pl.BlockSpec((1, 1, d), lambda i: (seg(i), 0, 5)),
            pl.BlockSpec((1, d), lambda i: (0, 0)),
        ],
        out_specs=pl.BlockSpec((tm, d), lambda i: (i, 0)),
        out_shape=jax.ShapeDtypeStruct((n, d), F32),
        compiler_params=_cparams("arbitrary"),
        name="moe_mix",
    )(yp.reshape(-1, 2 * d), xa, gate, modl, final_g.reshape(1, d))


def _slot_pairs(expert, blk):
    m = expert.size
    e_flat = expert.reshape(m)
    experts = jnp.arange(N_EXPERTS, dtype=jnp.int32)
    counts = jnp.sum((e_flat[:, None] == experts[None, :]).astype(jnp.int32), axis=0)
    order = jnp.argsort(e_flat, stable=True).astype(jnp.int32)
    first = jnp.cumsum(counts) - counts
    padded = (counts + blk - 1) // blk * blk
    pad_end = jnp.cumsum(padded)
    n_blocks = -(-m // blk) + N_EXPERTS
    block_start = jnp.arange(n_blocks, dtype=jnp.int32) * blk
    block_exp = jnp.minimum(jnp.sum((pad_end[None, :] <= block_start[:, None]).astype(jnp.int32), axis=1),
                            N_EXPERTS - 1)
    slot = jnp.arange(n_blocks * blk, dtype=jnp.int32)
    e_slot = jnp.repeat(block_exp, blk)
    j = slot - (pad_end - padded)[e_slot]
    valid = jnp.logical_and(j >= 0, j < counts[e_slot])
    pair = order[jnp.clip(first[e_slot] + j, 0, m - 1)]
    slot_pair = jnp.where(valid, pair, m + slot % blk).astype(jnp.int32)
    n_used = (pad_end[-1] // blk).astype(jnp.int32).reshape(1)
    return slot_pair, block_exp.astype(jnp.int32), n_used


def _route(logits):
    lg = logits[:, :N_GROUPS]
    le = logits[:, N_GROUPS:N_GROUPS + N_EXPERTS].reshape(-1, N_GROUPS, EXP_PER_GROUP)
    p_g = jax.nn.softmax(lg, axis=-1)
    g_sel = jnp.argmax(lg, axis=-1).astype(jnp.int32)
    le = jnp.take_along_axis(le, g_sel[:, None, None], axis=1)[:, 0]
    top_p, top_i = lax.top_k(jax.nn.softmax(le, axis=-1), 2)
    gate = jnp.max(p_g, axis=-1, keepdims=True) * top_p / jnp.sum(top_p, axis=-1, keepdims=True)
    expert = (g_sel[:, None] * EXP_PER_GROUP + top_i).astype(jnp.int32)
    return expert, gate


def _block_diag2(m):
    z = jnp.zeros_like(m[0])
    return jnp.concatenate([jnp.concatenate([m[0], z], axis=1), jnp.concatenate([z, m[1]], axis=1)], axis=0)


def kernel(x, c, ctx, c_ctx, w_mod, b_mod, norm1_g, norm2_g, w_in, shift_mu, conv_w, w_up, w0, a_up, a0, g_up, k_k, k_a, r_k, lnx_g, lnx_b, w_a_out, w_b_out, w_o, router_g, router_g_b, router_e, router_e_b, exp_w1, exp_w3, exp_w2, final_g):
    bsz, seq, d = x.shape
    ctx_len = ctx.shape[1]
    depth = w_mod.shape[0]
    da = w_a_out.shape[1]
    db = w_b_out.shape[1]
    heads = db // HEAD
    assert bsz * heads == LANES and da == db and seq % ctx_len == 0 and ctx_len % GRID_W == 0
    assert bsz + 1 <= MOD_ROWS
    nc_rows = bsz * ctx_len
    t_all = ctx_len + seq
    tm = ctx_len
    tm_mm = min(512, seq)
    tc = min(SCAN_STEPS, ctx_len)
    tiles = dict(tm=tm, n_ctx_tiles=bsz, tiles_per_seq=seq // tm)
    segs = dict(nc_rows=nc_rows, seq=seq)

    s_all = jnp.zeros((MOD_ROWS, d), F32).at[0].set(c_ctx).at[1:1 + bsz].set(c)
    mod = _modulation(s_all, w_mod, b_mod)
    xa = jnp.concatenate([ctx.reshape(nc_rows, d), x.reshape(bsz * seq, d)], axis=0)
    ones_bd = jnp.kron(jnp.eye(heads, dtype=F32), jnp.ones((HEAD, HEAD), F32)).astype(BF16)
    rw0 = 3 * da
    rw1 = rw0 + 3 * db + 3 * LANES

    for l in range(depth):
        modl = mod[l].reshape(MOD_ROWS, 1, 6 * d)
        w_in_b = w_in[l].astype(BF16)
        router_w = jnp.zeros((d, ROUTER_COLS), F32)
        router_w = router_w.at[:, :N_GROUPS].set(router_g[l]).at[:, N_GROUPS:N_GROUPS + N_EXPERTS].set(router_e[l])
        router_b = jnp.zeros((1, ROUTER_COLS), F32)
        router_b = router_b.at[0, :N_GROUPS].set(router_g_b[l]).at[0, N_GROUPS:N_GROUPS + N_EXPERTS].set(router_e_b[l])
        router_hi = router_w.astype(BF16)
        router_lo = (router_w - router_hi.astype(F32)).astype(BF16)
        lp = {
            "shift_mu": shift_mu[l], "k_k": k_k[l], "k_a": k_a[l], "r_k": r_k[l],
            "w0": w0[l], "a0": a0[l],
            "w_up_bd": _block_diag2(w_up[l]).astype(BF16), "a_up_bd": _block_diag2(a_up[l]).astype(BF16),
            "g_up": g_up[l].astype(BF16), "norm2_g": norm2_g[l], "conv_w": conv_w[l],
            "lnx_g": lnx_g[l], "lnx_b": lnx_b[l],
            "w_a_out": w_a_out[l].astype(BF16), "w_b_out": w_b_out[l].astype(BF16), "w_o": w_o[l].astype(BF16),
            "router_w": jnp.stack([router_hi, router_lo]), "router_b": router_b,
        }
        zc, zr, zg = _in_proj(xa, norm1_g[l].reshape(1, d), modl, w_in_b[:, :rw0], w_in_b[:, rw0:rw1],
                              w_in_b[:, rw1:], tm=tm_mm, **segs)
        p, q = _prep(zr, lp, ones_bd, **tiles)
        p_t = _relayout(p, bsz=bsz, ctx_len=ctx_len, seq=seq)
        y_t = _wkv_scan(p_t, ctx_len=ctx_len, tc=tc)
        y_bt = (y_t[0] + y_t[1]).reshape(t_all, HEAD, bsz, heads).transpose(2, 0, 3, 1)
        y = jnp.concatenate([y_bt[:, :ctx_len].reshape(nc_rows, db), y_bt[:, ctx_len:].reshape(bsz * seq, db)], axis=0)
        xa, h2, logits = _mix_out(xa, zc, y, q, zg, modl, lp, ones_bd, **tiles, **segs)
        expert, gate = _route(logits)
        slot_pair, block_exp, n_used = _slot_pairs(expert, MOE_BLOCK)
        yp = _moe_fused(h2, slot_pair, block_exp, n_used, exp_w1, exp_w3, exp_w2, l)
        xa = _moe_mix(yp, gate, xa, modl, final_g, final=(l == depth - 1), tm=tm, **segs)

    return xa[nc_rows:].reshape(bsz, seq, d)
```

```python
import functools

import jax
import jax.numpy as jnp
from jax import lax
from jax.experimental import pallas as pl
from jax.experimental.pallas import tpu as pltpu

F32 = jnp.float32
BF16 = jnp.bfloat16
HIGHEST = lax.Precision.HIGHEST

NORM_EPS = 1e-6
GN_EPS = 64e-5
KK_EPS = 1e-12

GRID_W = 64
HEAD = 64
N_GROUPS = 4
EXP_PER_GROUP = 8
N_EXPERTS = N_GROUPS * EXP_PER_GROUP
SUBLANES = 8
LANES = 128
MOD_ROWS = 24
ROUTER_COLS = LANES
SCAN_STEPS = 32
MOE_BLOCK = 256
N_SCAN_ARRAYS = 9
VMEM_LIMIT = 56 * 1024 * 1024


def _cparams(*sem):
    return pltpu.CompilerParams(dimension_semantics=sem, vmem_limit_bytes=VMEM_LIMIT)


def _sigmoid(x):
    return 1.0 / (1.0 + jnp.exp(-x))


def _dot_hi(a, b):
    return jnp.dot(a, b, precision=HIGHEST, preferred_element_type=F32)


def _dot_bf16(a, b):
    return jnp.dot(a.astype(BF16), b, preferred_element_type=F32)


def _split_bf16(x):
    hi = x.astype(BF16)
    lo = (x - hi.astype(F32)).astype(BF16)
    return hi, lo


def _sum_heads(x, ones_bd):
    hi, lo = _split_bf16(x)
    return (jnp.dot(hi, ones_bd, preferred_element_type=F32)
            + jnp.dot(lo, ones_bd, preferred_element_type=F32))


def _rms(x, g):
    ms = jnp.mean(x * x, axis=-1, keepdims=True)
    return x * lax.rsqrt(ms + NORM_EPS) * g


def _mod_kernel(s_ref, w_ref, b_ref, o_ref):
    s = s_ref[...]
    s = s * _sigmoid(s)
    o_ref[0] = _dot_hi(s, w_ref[0]) + b_ref[0]


def _modulation(s_all, w_mod, b_mod):
    depth, d, n6 = w_mod.shape
    tn = d
    return pl.pallas_call(
        _mod_kernel,
        grid=(depth, n6 // tn),
        in_specs=[
            pl.BlockSpec((MOD_ROWS, d), lambda l, j: (0, 0)),
            pl.BlockSpec((1, d, tn), lambda l, j: (l, 0, j)),
            pl.BlockSpec((1, 1, tn), lambda l, j: (l, 0, j)),
        ],
        out_specs=pl.BlockSpec((1, MOD_ROWS, tn), lambda l, j: (l, 0, j)),
        out_shape=jax.ShapeDtypeStruct((depth, MOD_ROWS, n6), F32),
        compiler_params=_cparams("arbitrary", "arbitrary"),
        name="modulation",
    )(s_all, w_mod, b_mod.reshape(depth, 1, n6))


def _seg_of_tile(i, tm, nc_rows, seq):
    return jnp.where(i * tm < nc_rows, 0, 1 + (i * tm - nc_rows) // seq)


def _in_proj_kernel(x_ref, g_ref, sh_ref, sc_ref, wc_ref, wr_ref, wg_ref, zc_ref, zr_ref, zg_ref):
    h = _rms(x_ref[...], g_ref[...]) * (1.0 + sc_ref[0]) + sh_ref[0]
    hb = h.astype(BF16)
    zc_ref[...] = jnp.dot(hb, wc_ref[...], preferred_element_type=F32)
    zr_ref[...] = jnp.dot(hb, wr_ref[...], preferred_element_type=F32)
    zg_ref[...] = jnp.dot(hb, wg_ref[...], preferred_element_type=F32)


def _in_proj(xa, norm_g, modl, w_conv, w_rw, w_gate, *, tm, nc_rows, seq):
    n, d = xa.shape
    seg = functools.partial(_seg_of_tile, tm=tm, nc_rows=nc_rows, seq=seq)
    const = lambda shape: pl.BlockSpec(shape, lambda i: (0, 0), pipeline_mode=pl.Buffered(1))
    nc_, nr_, ng_ = w_conv.shape[1], w_rw.shape[1], w_gate.shape[1]
    return pl.pallas_call(
        _in_proj_kernel,
        grid=(n // tm,),
        in_specs=[
            pl.BlockSpec((tm, d), lambda i: (i, 0)),
            pl.BlockSpec((1, d), lambda i: (0, 0)),
            pl.BlockSpec((1, 1, d), lambda i: (seg(i), 0, 0)),
            pl.BlockSpec((1, 1, d), lambda i: (seg(i), 0, 1)),
            const((d, nc_)), const((d, nr_)), const((d, ng_)),
        ],
        out_specs=[
            pl.BlockSpec((tm, nc_), lambda i: (i, 0)),
            pl.BlockSpec((tm, nr_), lambda i: (i, 0)),
            pl.BlockSpec((tm, ng_), lambda i: (i, 0)),
        ],
        out_shape=[
            jax.ShapeDtypeStruct((n, nc_), F32),
            jax.ShapeDtypeStruct((n, nr_), F32),
            jax.ShapeDtypeStruct((n, ng_), F32),
        ],
        compiler_params=_cparams("arbitrary"),
        name="in_proj",
    )(xa, norm_g, modl, modl, w_conv, w_rw, w_gate)


def _prep_kernel(z_ref, zp_ref, zn_ref, mu_ref, kk_w_ref, ka_ref, rk_ref, w0_ref, a0_ref,
                 wup_ref, aup_ref, gup_ref, ones_ref, p_ref, q_ref, *, tm, n_ctx_tiles, tiles_per_seq):
    i = pl.program_id(0)
    is_ctx = i < n_ctx_tiles
    j = (i - n_ctx_tiles) % tiles_per_seq
    first = jnp.logical_or(is_ctx, j == 0)
    last = jnp.logical_or(is_ctx, j == tiles_per_seq - 1)
    z = z_ref[...]
    row = lax.broadcasted_iota(jnp.int32, z.shape, 0)
    prev_row = jnp.where(first, 0.0, zp_ref[SUBLANES - 1:SUBLANES, :])
    next_row = jnp.where(last, 0.0, zn_ref[0:1, :])
    z_prev = jnp.where(row == 0, prev_row, pltpu.roll(z, 1, axis=0))
    z_next = jnp.where(row == tm - 1, next_row, pltpu.roll(z, tm - 1, axis=0))
    zs = z + mu_ref[...] * (0.5 * (z_prev + z_next) - z)

    db = ones_ref.shape[0]
    r = zs[:, 0:db]
    k = zs[:, db:2 * db]
    v = zs[:, 2 * db:3 * db]
    o = 3 * db
    zw = zs[:, o:o + LANES]
    za = zs[:, o + LANES:o + 2 * LANES]
    zg = zs[:, o + 2 * LANES:o + 3 * LANES]

    kk = k * kk_w_ref[...]
    kk = kk * lax.rsqrt(_sum_heads(kk * kk, ones_ref[...]) + KK_EPS)
    u = w0_ref[...] + _dot_bf16(jnp.tanh(zw), wup_ref[...])
    nu = -u
    softplus = jnp.maximum(nu, 0.0) + jnp.log(1.0 + jnp.exp(-jnp.abs(nu)))
    decay = jnp.exp(-jnp.exp(-softplus - 0.5))
    a = _sigmoid(a0_ref[...] + _dot_bf16(za, aup_ref[...]))
    ka = ka_ref[...]
    a_f, a_b = a[:, :db], a[:, db:]
    kd_f = k * (1.0 + (a_f - 1.0) * ka)
    kd_b = k * (1.0 + (a_b - 1.0) * ka)
    bonus = _sum_heads(r * (kd_f + kd_b) * rk_ref[...], ones_ref[...]) * v
    gate = _dot_bf16(_sigmoid(zg), gup_ref[...])

    p_ref[:, 0 * db:1 * db] = kk
    p_ref[:, 1 * db:2 * db] = r
    p_ref[:, 2 * db:3 * db] = decay[:, :db]
    p_ref[:, 3 * db:4 * db] = kd_f
    p_ref[:, 4 * db:5 * db] = kk * a_f
    p_ref[:, 5 * db:6 * db] = decay[:, db:]
    p_ref[:, 6 * db:7 * db] = kd_b
    p_ref[:, 7 * db:8 * db] = kk * a_b
    p_ref[:, 8 * db:9 * db] = v
    q_ref[:, 0:db] = bonus
    q_ref[:, db:2 * db] = gate


def _prep(zr, lp, ones_bd, *, tm, n_ctx_tiles, tiles_per_seq):
    n, nr = zr.shape
    db = ones_bd.shape[0]
    hb = tm // SUBLANES
    nblk8 = n // SUBLANES
    row = lambda a: a.reshape(1, -1)
    const = lambda a: pl.BlockSpec(a.shape, lambda i: (0,) * a.ndim)
    params = [row(lp["shift_mu"]), row(lp["k_k"]), row(lp["k_a"]), row(lp["r_k"]),
              row(lp["w0"]), row(lp["a0"]), lp["w_up_bd"], lp["a_up_bd"], lp["g_up"], ones_bd]
    kern = functools.partial(_prep_kernel, tm=tm, n_ctx_tiles=n_ctx_tiles, tiles_per_seq=tiles_per_seq)
    return pl.pallas_call(
        kern,
        grid=(n // tm,),
        in_specs=[
            pl.BlockSpec((tm, nr), lambda i: (i, 0)),
            pl.BlockSpec((SUBLANES, nr), lambda i: (jnp.maximum(i * hb - 1, 0), 0)),
            pl.BlockSpec((SUBLANES, nr), lambda i: (jnp.minimum((i + 1) * hb, nblk8 - 1), 0)),
        ] + [const(a) for a in params],
        out_specs=[
            pl.BlockSpec((tm, N_SCAN_ARRAYS * db), lambda i: (i, 0)),
            pl.BlockSpec((tm, 2 * db), lambda i: (i, 0)),
        ],
        out_shape=[
            jax.ShapeDtypeStruct((n, N_SCAN_ARRAYS * db), F32),
            jax.ShapeDtypeStruct((n, 2 * db), F32),
        ],
        compiler_params=_cparams("arbitrary"),
        name="rwkv_prep",
    )(zr, zr, zr, *params)


def _relayout_kernel(*refs):
    x_refs, o_ref, a_scr = refs[:-2], refs[-2], refs[-1]
    nk = o_ref.shape[0]
    tiles = x_refs[0].shape[1] // LANES
    for b, x_ref in enumerate(x_refs):
        for hp in range(tiles):
            a_scr[pl.ds((b * tiles + hp) * LANES, LANES), :] = x_ref[:, hp * LANES:(hp + 1) * LANES].T
    for k in range(nk):
        o_ref[k] = a_scr[pl.ds(k, LANES, stride=nk), :].T


def _relayout(p, *, bsz, ctx_len, seq):
    n, w_all = p.shape
    width = w_all // N_SCAN_ARRAYS
    heads = width // HEAD
    assert bsz * heads == LANES and ctx_len % LANES == 0 and seq % LANES == 0
    t_all = ctx_len + seq
    ncb = ctx_len // LANES
    nc_blocks = bsz * ncb

    def row_block(b):
        return lambda a, tb: (jnp.where(tb < ncb, b * ncb + tb, nc_blocks + b * (seq // LANES) + tb - ncb), a)

    return pl.pallas_call(
        _relayout_kernel,
        grid=(N_SCAN_ARRAYS, t_all // LANES),
        in_specs=[pl.BlockSpec((LANES, width), row_block(b)) for b in range(bsz)],
        out_specs=pl.BlockSpec((None, HEAD, LANES, LANES), lambda a, tb: (a, 0, tb, 0)),
        out_shape=jax.ShapeDtypeStruct((N_SCAN_ARRAYS, HEAD, t_all, LANES), F32),
        scratch_shapes=[pltpu.VMEM((LANES * HEAD, LANES), F32)],
        compiler_params=_cparams("arbitrary", "arbitrary"),
        name="scan_relayout",
    )(*([p] * bsz))


def _row_bcast(ref, t, k):
    return jnp.broadcast_to(ref[k, pl.ds(t, 1), :], (SUBLANES, LANES))


def _wkv_kernel(kk_ref, r_ref, v_ref, w_ref, kd_ref, b_ref, y_ref, s_scr, sa_scr, *, tc):
    d = pl.program_id(0)
    nk = s_scr.shape[0]
    nvb = s_scr.shape[1] // SUBLANES
    vrows = lambda vb: pl.ds(vb * SUBLANES, SUBLANES)

    @pl.when(pl.program_id(1) == 0)
    def _():
        s_scr[...] = jnp.zeros_like(s_scr)

    t_first = d * (tc - 1)
    acc = [jnp.zeros((SUBLANES, LANES), F32) for _ in range(nvb)]
    for k in range(nk):
        kk0 = _row_bcast(kk_ref, t_first, k)
        for vb in range(nvb):
            acc[vb] = acc[vb] + s_scr[k, vrows(vb), :] * kk0
    for vb in range(nvb):
        sa_scr[vrows(vb), :] = acc[vb]

    sublane = lax.broadcasted_iota(jnp.int32, (SUBLANES, LANES), 0)

    def step(i, carry):
        t = i + d * (tc - 1 - 2 * i)
        tn = jnp.clip(t + 1 - 2 * d, 0, tc - 1)
        sa = [sa_scr[vrows(vb), :] for vb in range(nvb)]
        vv = []
        for vb in range(nvb):
            tile = _row_bcast(v_ref, t, vb * SUBLANES)
            for j in range(1, SUBLANES):
                tile = jnp.where(sublane == j, _row_bcast(v_ref, t, vb * SUBLANES + j), tile)
            vv.append(tile)
        y = [jnp.zeros((SUBLANES, LANES), F32) for _ in range(nvb)]
        san = [jnp.zeros((SUBLANES, LANES), F32) for _ in range(nvb)]
        for k in range(nk):
            wk = _row_bcast(w_ref, t, k)
            bk = _row_bcast(b_ref, t, k)
            kdk = _row_bcast(kd_ref, t, k)
            rk = _row_bcast(r_ref, t, k)
            kkn = _row_bcast(kk_ref, tn, k)
            for vb in range(nvb):
                s = s_scr[k, vrows(vb), :]
                s = s * wk - sa[vb] * bk + vv[vb] * kdk
                s_scr[k, vrows(vb), :] = s
                y[vb] = y[vb] + s * rk
                san[vb] = san[vb] + s * kkn
        for vb in range(nvb):
            y_ref[t, vrows(vb), :] = y[vb]
            sa_scr[vrows(vb), :] = san[vb]
        return carry

    lax.fori_loop(0, tc, step, 0)


def _wkv_scan(p_t, *, ctx_len, tc):
    _, nk, t_all, lanes = p_t.shape
    nb = t_all // tc
    nbc = ctx_len // tc

    def blk(d, j):
        rev = jnp.where(j < nbc, nbc - 1 - j, nb - 1 - j + nbc)
        return jnp.where(d == 0, j, rev)

    shared = lambda idx: pl.BlockSpec((None, nk, tc, lanes), lambda d, j: (idx, 0, blk(d, j), 0))
    per_dir = lambda idx: pl.BlockSpec((None, nk, tc, lanes), lambda d, j: (idx + 3 * d, 0, blk(d, j), 0))
    return pl.pallas_call(
        functools.partial(_wkv_kernel, tc=tc),
        grid=(2, nb),
        in_specs=[shared(0), shared(1), shared(8), per_dir(2), per_dir(3), per_dir(4)],
        out_specs=pl.BlockSpec((None, tc, nk, lanes), lambda d, j: (d, blk(d, j), 0, 0)),
        out_shape=jax.ShapeDtypeStruct((2, t_all, nk, lanes), F32),
        scratch_shapes=[pltpu.VMEM((nk, nk, lanes), F32), pltpu.VMEM((nk, lanes), F32)],
        compiler_params=_cparams("arbitrary", "arbitrary"),
        name="wkv_scan",
    )(p_t, p_t, p_t, p_t, p_t, p_t)


def _mix_out_kernel(x_ref, zc_ref, zcp_ref, zcn_ref, y_ref, q_ref, zg_ref,
                    g1_ref, sh2_ref, sc2_ref, n2_ref, cw_ref, lng_ref, lnb_ref, ones_ref,
                    wa_ref, wb_ref, wo_ref, wr_ref, br_ref,
                    xo_ref, h2_ref, lg_ref, *, tm, n_ctx_tiles, tiles_per_seq):
    i = pl.program_id(0)
    is_ctx = i < n_ctx_tiles
    j = (i - n_ctx_tiles) % tiles_per_seq
    first = jnp.logical_or(is_ctx, j == 0)
    last = jnp.logical_or(is_ctx, j == tiles_per_seq - 1)
    da = ones_ref.shape[0]
    half = da // 2

    zc = zc_ref[...]
    bg, u = zc[:, 0:da], zc[:, da:2 * da] * zc[:, 2 * da:3 * da]
    row = lax.broadcasted_iota(jnp.int32, u.shape, 0)
    period = jnp.where(is_ctx, tm, GRID_W)
    pos = jnp.bitwise_and(row, period - 1)
    u_m1 = jnp.where(pos == 0, 0.0, pltpu.roll(u, 1, axis=0))
    u_p1 = jnp.where(pos == period - 1, 0.0, pltpu.roll(u, tm - 1, axis=0))
    uh = u[:, half:]
    hp = zcp_ref[:, da + half:2 * da] * zcp_ref[:, 2 * da + half:3 * da]
    hn = zcn_ref[:, da + half:2 * da] * zcn_ref[:, 2 * da + half:3 * da]
    hp = jnp.where(first, 0.0, hp)
    hn = jnp.where(last, 0.0, hn)
    uv_m = jnp.concatenate([hp, uh[:tm - GRID_W]], axis=0)
    uv_p = jnp.concatenate([uh[GRID_W:], hn], axis=0)
    um_hi = jnp.where(is_ctx, u_m1[:, half:], uv_m)
    up_hi = jnp.where(is_ctx, u_p1[:, half:], uv_p)
    cw = cw_ref[...]
    conv_lo = u_m1[:, :half] * cw[0:1, :half] + u[:, :half] * cw[1:2, :half] + u_p1[:, :half] * cw[2:3, :half]
    conv_hi = um_hi * cw[0:1, half:] + uh * cw[1:2, half:] + up_hi * cw[2:3, half:]
    ya = (_dot_bf16(bg[:, :half] * conv_lo, wa_ref[0:half, :])
          + _dot_bf16(bg[:, half:] * conv_hi, wa_ref[half:, :]))

    y = y_ref[...]
    inv_n = 1.0 / HEAD
    mean = _sum_heads(y, ones_ref[...]) * inv_n
    yc = y - mean
    var = _sum_heads(yc * yc, ones_ref[...]) * inv_n
    yn = yc * lax.rsqrt(var + GN_EPS) * lng_ref[...] + lnb_ref[...]
    db = y.shape[1]
    yb = _dot_bf16((yn + q_ref[:, 0:db]) * q_ref[:, db:2 * db], wb_ref[...])

    d = x_ref.shape[1]
    merged = _sigmoid(zg_ref[:, 0:d]) * ya + _sigmoid(zg_ref[:, d:2 * d]) * yb
    x_new = x_ref[...] + g1_ref[0] * _dot_bf16(merged, wo_ref[...])
    xo_ref[...] = x_new

    h2 = _rms(x_new, n2_ref[...]) * (1.0 + sc2_ref[0]) + sh2_ref[0]
    h2_ref[...] = h2
    h_hi, h_lo = _split_bf16(h2)
    lg_ref[...] = (jnp.dot(h_hi, wr_ref[0], preferred_element_type=F32)
                   + jnp.dot(h_lo, wr_ref[0], preferred_element_type=F32)
                   + jnp.dot(h_hi, wr_ref[1], preferred_element_type=F32)) + br_ref[...]


def _mix_out(xa, zc, y, q, zg, modl, lp, ones_bd, *, tm, n_ctx_tiles, tiles_per_seq, nc_rows, seq):
    n, d = xa.shape
    da3 = zc.shape[1]
    db = y.shape[1]
    hb = tm // GRID_W
    nblk = n // GRID_W
    seg = functools.partial(_seg_of_tile, tm=tm, nc_rows=nc_rows, seq=seq)
    mod = lambda c: pl.BlockSpec((1, 1, d), lambda i: (seg(i), 0, c))
    const = lambda a: pl.BlockSpec(a.shape, lambda i: (0,) * a.ndim)
    tile = lambda w: pl.BlockSpec((tm, w), lambda i: (i, 0))
    params = [lp["norm2_g"].reshape(1, d), lp["conv_w"], lp["lnx_g"].reshape(1, db), lp["lnx_b"].reshape(1, db),
              ones_bd, lp["w_a_out"], lp["w_b_out"], lp["w_o"], lp["router_w"], lp["router_b"]]
    kern = functools.partial(_mix_out_kernel, tm=tm, n_ctx_tiles=n_ctx_tiles, tiles_per_seq=tiles_per_seq)
    return pl.pallas_call(
        kern,
        grid=(n // tm,),
        in_specs=[
            tile(d), tile(da3),
            pl.BlockSpec((GRID_W, da3), lambda i: (jnp.maximum(i * hb - 1, 0), 0)),
            pl.BlockSpec((GRID_W, da3), lambda i: (jnp.minimum((i + 1) * hb, nblk - 1), 0)),
            tile(db), tile(2 * db), tile(2 * d),
            mod(2), mod(3), mod(4),
        ] + [const(a) for a in params],
        out_specs=[tile(d), tile(d), tile(ROUTER_COLS)],
        out_shape=[
            jax.ShapeDtypeStruct((n, d), F32),
            jax.ShapeDtypeStruct((n, d), F32),
            jax.ShapeDtypeStruct((n, ROUTER_COLS), F32),
        ],
        compiler_params=_cparams("arbitrary"),
        name="mix_out",
    )(xa, zc, zc, zc, y, q, zg, modl, modl, modl, *params)


def _moe_fused_kernel(bexp_ref, nused_ref, spair_ref, h_hbm, w1_ref, w3_ref, w2_ref, yp_hbm,
                      hbuf, obuf, w1b, w3b, w2b, gsem, ssem, *, n_tok):
    i = pl.program_id(0)
    n_used = nused_ref[0]
    n_blocks = pl.num_programs(0)
    blk = hbuf.shape[1]

    def gather(block, slot, r):
        tok = jnp.minimum(lax.shift_right_logical(spair_ref[block * blk + r], 1), n_tok - 1)
        return pltpu.make_async_copy(h_hbm.at[pl.ds(tok, 1)], hbuf.at[slot, pl.ds(r, 1)], gsem.at[slot])

    def scatter(block, slot, r):
        pair = spair_ref[block * blk + r]
        dst = yp_hbm.at[lax.bitwise_and(pair, 1), pl.ds(lax.shift_right_logical(pair, 1), 1)]
        return pltpu.make_async_copy(obuf.at[slot, pl.ds(r, 1)], dst, ssem.at[slot])

    def wait_gather(slot):
        for r in range(blk):
            pltpu.make_async_copy(h_hbm.at[pl.ds(0, 1)], hbuf.at[slot, pl.ds(r, 1)], gsem.at[slot]).wait()

    def wait_scatter(slot):
        for r in range(blk):
            pltpu.make_async_copy(obuf.at[slot, pl.ds(r, 1)], yp_hbm.at[0, pl.ds(0, 1)], ssem.at[slot]).wait()

    @pl.when(i == 0)
    def _():
        for r in range(blk):
            gather(0, 0, r).start()
        obuf[1] = jnp.zeros(obuf.shape[1:], obuf.dtype)

    @pl.when(jnp.logical_or(i == 0, bexp_ref[i] != bexp_ref[jnp.maximum(i - 1, 0)]))
    def _():
        w1b[...] = w1_ref[0].astype(BF16)
        w3b[...] = w3_ref[0].astype(BF16)
        w2b[...] = w2_ref[0].astype(BF16)

    def run_block(s):
        wait_gather(s)

        @pl.when(i >= 1)
        def _():
            wait_scatter(s)

        nxt = jnp.minimum(i + 1, n_blocks - 1)
        prev = jnp.maximum(i - 1, 0)
        for r in range(blk):
            gather(nxt, 1 - s, r).start()
            scatter(prev, 1 - s, r).start()
        h = hbuf[s].astype(BF16)
        a = jnp.dot(h, w1b[...], preferred_element_type=F32)
        b = jnp.dot(h, w3b[...], preferred_element_type=F32)
        hid = a * _sigmoid(a) * b
        obuf[s] = _dot_bf16(hid, w2b[...])

    for s in range(2):
        @pl.when(jnp.logical_and(i < n_used, lax.rem(i, 2) == s))
        def _(s=s):
            run_block(s)

    for s in range(2):
        @pl.when(jnp.logical_and(i == n_used, lax.rem(i, 2) == s))
        def _(s=s):
            wait_gather(s)
            wait_scatter(s)
            for r in range(blk):
                scatter(i - 1, 1 - s, r).start()
            wait_scatter(1 - s)
            half = blk // 2
            hbuf[s, pl.ds(0, half), :] = jnp.zeros((half, hbuf.shape[2]), hbuf.dtype)
            for c in range(2):
                fill = pltpu.make_async_copy(hbuf.at[s, pl.ds(0, half)], yp_hbm.at[c, pl.ds(n_tok, half)],
                                             gsem.at[s])
                fill.start()
                fill.wait()


def _moe_fused(h2, slot_pair, block_exp, n_used, w1_all, w3_all, w2_all, layer):
    n, d = h2.shape
    f = w1_all.shape[-1]
    blk = MOE_BLOCK
    n_slots = slot_pair.shape[0]
    wspec = lambda shape: pl.BlockSpec((None, 1) + shape, lambda i, bexp, nused, sp: (layer, bexp[i], 0, 0))
    return pl.pallas_call(
        functools.partial(_moe_fused_kernel, n_tok=n),
        grid_spec=pltpu.PrefetchScalarGridSpec(
            num_scalar_prefetch=3,
            grid=(n_slots // blk,),
            in_specs=[pl.BlockSpec(memory_space=pl.ANY), wspec((d, f)), wspec((d, f)), wspec((f, d))],
            out_specs=pl.BlockSpec(memory_space=pl.ANY),
            scratch_shapes=[
                pltpu.VMEM((2, blk, d), F32), pltpu.VMEM((2, blk, d), F32),
                pltpu.VMEM((d, f), BF16), pltpu.VMEM((d, f), BF16), pltpu.VMEM((f, d), BF16),
                pltpu.SemaphoreType.DMA((2,)), pltpu.SemaphoreType.DMA((2,)),
            ],
        ),
        out_shape=jax.ShapeDtypeStruct((2, n + blk // 2, d), F32),
        compiler_params=_cparams("arbitrary"),
        name="moe_experts",
    )(block_exp, n_used, slot_pair, h2, w1_all, w3_all, w2_all)


def _moe_mix_kernel(y0_ref, y1_ref, x_ref, gate_ref, g2_ref, fg_ref, o_ref, *, final):
    gate = gate_ref[...]
    y2 = gate[:, 0:1] * y0_ref[...] + gate[:, 1:2] * y1_ref[...]
    x_new = x_ref[...] + g2_ref[0] * y2
    o_ref[...] = _rms(x_new, fg_ref[...]) if final else x_new


def _moe_mix(yp, gate, xa, modl, final_g, *, final, tm, nc_rows, seq):
    n, d = xa.shape
    seg = functools.partial(_seg_of_tile, tm=tm, nc_rows=nc_rows, seq=seq)
    return pl.pallas_call(
        functools.partial(_moe_mix_kernel, final=final),
        grid=(n // tm,),
        in_specs=[
            pl.BlockSpec((None, tm, d), lambda i: (0, i, 0)),
            pl.BlockSpec((None, tm, d), lambda i: (1, i, 0)),
            pl.BlockSpec((tm, d), lambda i: (i, 0)),
            pl.BlockSpec((tm, 2), lambda i: (i, 0)),
            pl.BlockSpec((1, 1, d), lambda i: (seg(i), 0, 5)),
            pl.BlockSpec((1, d), lambda i: (0, 0)),
        ],
        out_specs=pl.BlockSpec((tm, d), lambda i: (i, 0)),
        out_shape=jax.ShapeDtypeStruct((n, d), F32),
        compiler_params=_cparams("arbitrary"),
        name="moe_mix",
    )(yp, yp, xa, gate, modl, final_g.reshape(1, d))


def _slot_pairs(expert, blk):
    m = expert.size
    e_flat = expert.reshape(m)
    experts = jnp.arange(N_EXPERTS, dtype=jnp.int32)
    counts = jnp.sum((e_flat[:, None] == experts[None, :]).astype(jnp.int32), axis=0)
    order = jnp.argsort(e_flat, stable=True).astype(jnp.int32)
    first = jnp.cumsum(counts) - counts
    padded = (counts + blk - 1) // blk * blk
    pad_end = jnp.cumsum(padded)
    n_blocks = -(-m // blk) + N_EXPERTS
    block_start = jnp.arange(n_blocks, dtype=jnp.int32) * blk
    block_exp = jnp.minimum(jnp.sum((pad_end[None, :] <= block_start[:, None]).astype(jnp.int32), axis=1),
                            N_EXPERTS - 1)
    slot = jnp.arange(n_blocks * blk, dtype=jnp.int32)
    e_slot = jnp.repeat(block_exp, blk)
    j = slot - (pad_end - padded)[e_slot]
    valid = jnp.logical_and(j >= 0, j < counts[e_slot])
    pair = order[jnp.clip(first[e_slot] + j, 0, m - 1)]
    slot_pair = jnp.where(valid, pair, m + slot % blk).astype(jnp.int32)
    n_used = (pad_end[-1] // blk).astype(jnp.int32).reshape(1)
    return slot_pair, block_exp.astype(jnp.int32), n_used


def _route(logits):
    lg = logits[:, :N_GROUPS]
    le = logits[:, N_GROUPS:N_GROUPS + N_EXPERTS].reshape(-1, N_GROUPS, EXP_PER_GROUP)
    p_g = jax.nn.softmax(lg, axis=-1)
    g_sel = jnp.argmax(lg, axis=-1).astype(jnp.int32)
    le = jnp.take_along_axis(le, g_sel[:, None, None], axis=1)[:, 0]
    top_p, top_i = lax.top_k(jax.nn.softmax(le, axis=-1), 2)
    gate = jnp.max(p_g, axis=-1, keepdims=True) * top_p / jnp.sum(top_p, axis=-1, keepdims=True)
    expert = (g_sel[:, None] * EXP_PER_GROUP + top_i).astype(jnp.int32)
    return expert, gate


def _block_diag2(m):
    z = jnp.zeros_like(m[0])
    return jnp.concatenate([jnp.concatenate([m[0], z], axis=1), jnp.concatenate([z, m[1]], axis=1)], axis=0)


def kernel(x, c, ctx, c_ctx, w_mod, b_mod, norm1_g, norm2_g, w_in, shift_mu, conv_w, w_up, w0, a_up, a0, g_up, k_k, k_a, r_k, lnx_g, lnx_b, w_a_out, w_b_out, w_o, router_g, router_g_b, router_e, router_e_b, exp_w1, exp_w3, exp_w2, final_g):
    bsz, seq, d = x.shape
    ctx_len = ctx.shape[1]
    depth = w_mod.shape[0]
    da = w_a_out.shape[1]
    db = w_b_out.shape[1]
    heads = db // HEAD
    assert bsz * heads == LANES and da == db and seq % ctx_len == 0 and ctx_len % GRID_W == 0
    assert bsz + 1 <= MOD_ROWS
    nc_rows = bsz * ctx_len
    t_all = ctx_len + seq
    tm = ctx_len
    tm_mm = min(512, seq)
    tc = min(SCAN_STEPS, ctx_len)
    tiles = dict(tm=tm, n_ctx_tiles=bsz, tiles_per_seq=seq // tm)
    segs = dict(nc_rows=nc_rows, seq=seq)

    s_all = jnp.zeros((MOD_ROWS, d), F32).at[0].set(c_ctx).at[1:1 + bsz].set(c)
    mod = _modulation(s_all, w_mod, b_mod)
    xa = jnp.concatenate([ctx.reshape(nc_rows, d), x.reshape(bsz * seq, d)], axis=0)
    ones_bd = jnp.kron(jnp.eye(heads, dtype=F32), jnp.ones((HEAD, HEAD), F32)).astype(BF16)
    rw0 = 3 * da
    rw1 = rw0 + 3 * db + 3 * LANES

    for l in range(depth):
        modl = mod[l].reshape(MOD_ROWS, 1, 6 * d)
        w_in_b = w_in[l].astype(BF16)
        router_w = jnp.zeros((d, ROUTER_COLS), F32)
        router_w = router_w.at[:, :N_GROUPS].set(router_g[l]).at[:, N_GROUPS:N_GROUPS + N_EXPERTS].set(router_e[l])
        router_b = jnp.zeros((1, ROUTER_COLS), F32)
        router_b = router_b.at[0, :N_GROUPS].set(router_g_b[l]).at[0, N_GROUPS:N_GROUPS + N_EXPERTS].set(router_e_b[l])
        router_hi = router_w.astype(BF16)
        router_lo = (router_w - router_hi.astype(F32)).astype(BF16)
        lp = {
            "shift_mu": shift_mu[l], "k_k": k_k[l], "k_a": k_a[l], "r_k": r_k[l],
            "w0": w0[l], "a0": a0[l],
            "w_up_bd": _block_diag2(w_up[l]).astype(BF16), "a_up_bd": _block_diag2(a_up[l]).astype(BF16),
            "g_up": g_up[l].astype(BF16), "norm2_g": norm2_g[l], "conv_w": conv_w[l],
            "lnx_g": lnx_g[l], "lnx_b": lnx_b[l],
            "w_a_out": w_a_out[l].astype(BF16), "w_b_out": w_b_out[l].astype(BF16), "w_o": w_o[l].astype(BF16),
            "router_w": jnp.stack([router_hi, router_lo]), "router_b": router_b,
        }
        zc, zr, zg = _in_proj(xa, norm1_g[l].reshape(1, d), modl, w_in_b[:, :rw0], w_in_b[:, rw0:rw1],
                              w_in_b[:, rw1:], tm=tm_mm, **segs)
        p, q = _prep(zr, lp, ones_bd, **tiles)
        p_t = _relayout(p, bsz=bsz, ctx_len=ctx_len, seq=seq)
        y_t = _wkv_scan(p_t, ctx_len=ctx_len, tc=tc)
        y_bt = (y_t[0] + y_t[1]).reshape(t_all, HEAD, bsz, heads).transpose(2, 0, 3, 1)
        y = jnp.concatenate([y_bt[:, :ctx_len].reshape(nc_rows, db), y_bt[:, ctx_len:].reshape(bsz * seq, db)], axis=0)
        xa, h2, logits = _mix_out(xa, zc, y, q, zg, modl, lp, ones_bd, **tiles, **segs)
        expert, gate = _route(logits)
        slot_pair, block_exp, n_used = _slot_pairs(expert, MOE_BLOCK)
        yp = _moe_fused(h2, slot_pair, block_exp, n_used, exp_w1, exp_w3, exp_w2, l)
        xa = _moe_mix(yp, gate, xa, modl, final_g, final=(l == depth - 1), tm=tm, **segs)

    return xa[nc_rows:].reshape(bsz, seq, d)
```
